```python
import jax, jax.numpy as jnp
from jax import lax
import numpy as np

D_MODEL = 1024
BATCH = 1
SEQ = 16384
DEPTH = 2
DEC_BATCH = 8
DEC_SEQ = 2048
PAST_LEN = 128

N_MIXERS = 2
N_LRU_LAYERS = (DEPTH + 1) // 2
N_ATTN_LAYERS = DEPTH // 2
D_FF = 2816
N_FFN_PER_LAYER = 2
LRU_WIDTH = D_MODEL
LRU_BLOCKS = 8
LRU_BLOCK_W = LRU_WIDTH // LRU_BLOCKS
CONV_W = 4
CONV_LEFT = 2
LRU_C = 8.0
N_DIRS = 2
N_LRU_GATES = 2
HEAD_DIM = 128
N_HEADS = D_MODEL // HEAD_DIM
GROUPS = ((128, 1), (512, 4), (2048, 16))
N_GROUPS = len(GROUPS)
ATTN_BLOCK = 64
ROPE_THETA = 10000.0
EPS = 1e-6

kernel_name = "hybrid_rglru_dilated_swa_encoder"


def _rmsnorm(x, g):
    xf = x.astype(jnp.float32)
    y = xf * lax.rsqrt(jnp.mean(xf * xf, axis=-1, keepdims=True) + EPS)
    return (y * g.astype(jnp.float32)).astype(x.dtype)


def _swiglu(x, w_in, w_out):
    gate, up = jnp.split(x @ w_in, 2, axis=-1)
    return (jax.nn.silu(gate) * up) @ w_out


def _linear_scan(a, b, reverse):
    def combine(l, r):
        return (l[0] * r[0], r[0] * l[1] + r[1])
    _, h = lax.associative_scan(combine, (a, b), axis=1, reverse=reverse)
    return h


def _rglru_block(h, w_in, b_in, conv_w, conv_b, gate_w, gate_b, lam, w_out, b_out):
    B, S, _ = h.shape
    y, u = jnp.split(h @ w_in + b_in, 2, axis=-1)
    y = jax.nn.gelu(y)
    up = jnp.pad(u, ((0, 0), (CONV_LEFT, CONV_W - 1 - CONV_LEFT), (0, 0)))
    u = conv_b + sum(up[:, k:k + S] * conv_w[k] for k in range(CONV_W))
    ub = u.reshape(B, S, LRU_BLOCKS, LRU_BLOCK_W)
    gates = jnp.einsum('bsni,dgnij->dgbsnj', ub, gate_w) + gate_b[:, :, None, None]
    gates = jax.nn.sigmoid(gates.astype(jnp.float32)).reshape(N_DIRS, N_LRU_GATES, B, S, LRU_WIDTH)
    r, ig = gates[:, 0], gates[:, 1]
    log_a = -LRU_C * r * jax.nn.softplus(-lam.astype(jnp.float32))[:, None, None, :]
    a = jnp.exp(log_a)
    bterm = jnp.sqrt(-jnp.expm1(2.0 * log_a)) * ig * u.astype(jnp.float32)[None]
    h_f = _linear_scan(a[0], bterm[0], reverse=False)
    h_b = _linear_scan(a[1], bterm[1], reverse=True)
    return ((h_f + h_b).astype(h.dtype) * y) @ w_out + b_out


def _rope(t):
    S = t.shape[1]
    half = HEAD_DIM // 2
    inv = ROPE_THETA ** (-2.0 * jnp.arange(half, dtype=jnp.float32) / HEAD_DIM)
    ang = jnp.arange(S, dtype=jnp.float32)[:, None] * inv[None, :]
    cos = jnp.cos(ang)[:, None, None, :]
    sin = jnp.sin(ang)[:, None, None, :]
    tf = t.astype(jnp.float32)
    t1, t2 = tf[..., :half], tf[..., half:]
    return jnp.concatenate([t1 * cos - t2 * sin, t2 * cos + t1 * sin], axis=-1).astype(t.dtype)


def _dilated_band(q, k, v, dil, reach):
    B, S, H, Dh = q.shape
    L = S // dil
    BLK = ATTN_BLOCK
    nb = -(-L // BLK)
    pad = nb * BLK - L

    def strided(t):
        return t.reshape(B, L, dil, H, Dh).transpose(0, 2, 1, 3, 4)

    qs, ks, vs = strided(q), strided(k), strided(v)
    qb = jnp.pad(qs, ((0, 0), (0, 0), (0, pad), (0, 0), (0, 0))).reshape(B, dil, nb, BLK, H, Dh)

    def key_windows(t):
        tp = jnp.pad(t, ((0, 0), (0, 0), (BLK, BLK + pad), (0, 0), (0, 0)))
        tb = tp.reshape(B, dil, nb + 2, BLK, H, Dh)
        return jnp.concatenate([tb[:, :, :-2], tb[:, :, 1:-1], tb[:, :, 2:]], axis=3)

    kw, vw = key_windows(ks), key_windows(vs)
    qi = jnp.arange(BLK)[:, None]
    km = jnp.arange(3 * BLK)[None, :]
    band = jnp.abs(km - BLK - qi) <= reach
    kidx = jnp.arange(nb)[:, None, None] * BLK - BLK + km[None]
    valid = band[None] & (kidx >= 0) & (kidx < L)

    s = jnp.einsum('brnqhe,brnkhe->brnhqk', qb, kw, preferred_element_type=jnp.float32)
    s = jnp.where(valid[:, None], s, -jnp.inf)
    lse = jax.nn.logsumexp(s, axis=-1)
    p = jnp.exp(s - lse[..., None])
    o = jnp.einsum('brnhqk,brnkhe->brnqhe', p, vw.astype(jnp.float32))
    o = o.reshape(B, dil, nb * BLK, H, Dh)[:, :, :L].transpose(0, 2, 1, 3, 4).reshape(B, S, H, Dh)
    lse = lse.transpose(0, 1, 2, 4, 3).reshape(B, dil, nb * BLK, H)[:, :, :L]
    lse = lse.transpose(0, 2, 1, 3).reshape(B, S, H)
    return o, lse


def _dilated_attention(h, w_qkv, q_gain, k_gain, w_o):
    B, S, _ = h.shape
    qkv = (h @ w_qkv).reshape(B, S, N_GROUPS, 3, N_HEADS, HEAD_DIM)
    q = _rmsnorm(qkv[:, :, :, 0], q_gain[:, None, :])
    k = _rmsnorm(qkv[:, :, :, 1], k_gain[:, None, :])
    v = qkv[:, :, :, 2]
    q = _rope(q) * (HEAD_DIM ** -0.5)
    k = _rope(k)
    outs, lses = [], []
    for gi, (window, dil) in enumerate(GROUPS):
        o, l = _dilated_band(q[:, :, gi], k[:, :, gi], v[:, :, gi], dil, window // (2 * dil))
        outs.append(o)
        lses.append(l)
    w = jax.nn.softmax(jnp.stack(lses), axis=0)
    o = jnp.einsum('gbsh,gbshe->bshe', w, jnp.stack(outs))
    return o.reshape(B, S, N_HEADS * HEAD_DIM).astype(h.dtype) @ w_o


def _trunk(x, norm_g, ffn_w_in, ffn_w_out, lru_w_in, lru_b_in, lru_conv_w, lru_conv_b,
           lru_gate_w, lru_gate_b, lru_lambda, lru_w_out, lru_b_out,
           attn_w_qkv, attn_q_gain, attn_k_gain, attn_w_o):
    for i in range(DEPTH):
        x = x + 0.5 * _swiglu(_rmsnorm(x, norm_g[i, 0]), ffn_w_in[i, 0], ffn_w_out[i, 0])
        hn = _rmsnorm(x, norm_g[i, 1])
        j = i // N_MIXERS
        if i % N_MIXERS == 0:
            x = x + _rglru_block(hn, lru_w_in[j], lru_b_in[j], lru_conv_w[j], lru_conv_b[j],
                                 lru_gate_w[j], lru_gate_b[j], lru_lambda[j], lru_w_out[j], lru_b_out[j])
        else:
            x = x + _dilated_attention(hn, attn_w_qkv[j], attn_q_gain[j], attn_k_gain[j], attn_w_o[j])
        x = x + 0.5 * _swiglu(_rmsnorm(x, norm_g[i, 2]), ffn_w_in[i, 1], ffn_w_out[i, 1])
    return x


def setup_inputs(seed: int = 0) -> dict:
    key = jax.random.key(seed)
    ks = jax.random.split(key, 20)
    f32 = jnp.float32
    nrm = lambda k, shape, scale: jax.random.normal(k, shape, f32) * scale
    a0 = jax.random.uniform(ks[10], (N_LRU_LAYERS, N_DIRS, LRU_WIDTH), f32, 0.9, 0.999)
    return {
        "x_prompt": nrm(ks[0], (BATCH, SEQ, D_MODEL), 1.0),
        "x_sample": nrm(ks[1], (DEC_BATCH, DEC_SEQ, D_MODEL), 1.0),
        "norm_g": 1.0 + nrm(ks[2], (DEPTH, 3, D_MODEL), 0.02),
        "ffn_w_in": nrm(ks[3], (DEPTH, N_FFN_PER_LAYER, D_MODEL, 2 * D_FF), D_MODEL ** -0.5),
        "ffn_w_out": nrm(ks[4], (DEPTH, N_FFN_PER_LAYER, D_FF, D_MODEL), D_FF ** -0.5),
        "lru_w_in": nrm(ks[5], (N_LRU_LAYERS, D_MODEL, 2 * LRU_WIDTH), D_MODEL ** -0.5),
        "lru_b_in": nrm(ks[6], (N_LRU_LAYERS, 2 * LRU_WIDTH), 0.01),
        "lru_conv_w": nrm(ks[7], (N_LRU_LAYERS, CONV_W, LRU_WIDTH), CONV_W ** -0.5),
        "lru_conv_b": nrm(ks[8], (N_LRU_LAYERS, LRU_WIDTH), 0.01),
        "lru_gate_w": nrm(ks[9], (N_LRU_LAYERS, N_DIRS, N_LRU_GATES, LRU_BLOCKS, LRU_BLOCK_W, LRU_BLOCK_W), LRU_BLOCK_W ** -0.5),
        "lru_gate_b": nrm(ks[11], (N_LRU_LAYERS, N_DIRS, N_LRU_GATES, LRU_BLOCKS, LRU_BLOCK_W), 0.01),
        "lru_lambda": jnp.log(a0) - jnp.log1p(-a0),
        "lru_w_out": nrm(ks[12], (N_LRU_LAYERS, LRU_WIDTH, D_MODEL), LRU_WIDTH ** -0.5),
        "lru_b_out": nrm(ks[13], (N_LRU_LAYERS, D_MODEL), 0.01),
        "attn_w_qkv": nrm(ks[14], (N_ATTN_LAYERS, D_MODEL, N_GROUPS * 3 * N_HEADS * HEAD_DIM), D_MODEL ** -0.5),
        "attn_q_gain": 1.0 + nrm(ks[15], (N_ATTN_LAYERS, N_GROUPS, HEAD_DIM), 0.02),
        "attn_k_gain": 1.0 + nrm(ks[16], (N_ATTN_LAYERS, N_GROUPS, HEAD_DIM), 0.02),
        "attn_w_o": nrm(ks[17], (N_ATTN_LAYERS, N_HEADS * HEAD_DIM, D_MODEL), (N_HEADS * HEAD_DIM) ** -0.5),
    }


def reference(x_prompt, x_sample, norm_g, ffn_w_in, ffn_w_out, lru_w_in, lru_b_in, lru_conv_w,
              lru_conv_b, lru_gate_w, lru_gate_b, lru_lambda, lru_w_out, lru_b_out,
              attn_w_qkv, attn_q_gain, attn_k_gain, attn_w_o):
    y_prompt = _trunk(x_prompt, norm_g, ffn_w_in, ffn_w_out, lru_w_in, lru_b_in, lru_conv_w, lru_conv_b,
                      lru_gate_w, lru_gate_b, lru_lambda, lru_w_out, lru_b_out,
                      attn_w_qkv, attn_q_gain, attn_k_gain, attn_w_o)
    y_sample = _trunk(x_sample, norm_g, ffn_w_in, ffn_w_out, lru_w_in, lru_b_in, lru_conv_w, lru_conv_b,
                      lru_gate_w, lru_gate_b, lru_lambda, lru_w_out, lru_b_out,
                      attn_w_qkv, attn_q_gain, attn_k_gain, attn_w_o)
    return (y_prompt, y_sample)
```

```python
import functools

import jax
import jax.numpy as jnp
from jax import lax
from jax.experimental import pallas as pl
from jax.experimental.pallas import tpu as pltpu

F32 = jnp.float32
BF16 = jnp.bfloat16

D_MODEL = 1024
D_FF = 2816
LRU_WIDTH = 1024
LRU_BLOCKS = 8
LRU_BLOCK_W = LRU_WIDTH // LRU_BLOCKS
CONV_W = 4
CONV_LEFT = 2
LRU_C = 8.0
HEAD_DIM = 128
N_HEADS = D_MODEL // HEAD_DIM
GROUPS = ((128, 1), (512, 4), (2048, 16))
N_GROUPS = len(GROUPS)
ROPE_THETA = 10000.0
EPS = 1e-6

SUBLANES = 8
LANES = 128
MXU_DIM = 256
VMEM_LIMIT = 56 * 1024 * 1024

FFN_TM = 512
FFN_TF = MXU_DIM
FFN_NC = D_FF // FFN_TF
LRU_TL = 512
QKV_TM = 256
ATTN_REACH = 64
ATTN_TL = 128
MERGE_TM = 512
NEG_BIG = -1e30


def _resident(shape):
    nd = len(shape)
    return pl.BlockSpec(shape, lambda *_: (0,) * nd, pipeline_mode=pl.Buffered(1))


def _rmsnorm(x, g):
    return x * lax.rsqrt(jnp.mean(x * x, axis=-1, keepdims=True) + EPS) * g


def _ffn_kernel(x_ref, g_ref, wgu_ref, wo_ref, o_ref, xn_ref, acc_ref):
    x = x_ref[...]
    xn_ref[...] = _rmsnorm(x, g_ref[...]).astype(BF16)
    acc_ref[...] = jnp.zeros_like(acc_ref)

    def chunk(c, carry):
        gu = jnp.dot(xn_ref[...], wgu_ref[c], preferred_element_type=F32)
        gate = gu[:, :FFN_TF]
        h = (gate * jax.nn.sigmoid(gate) * gu[:, FFN_TF:]).astype(BF16)
        acc_ref[...] += jnp.dot(h, wo_ref[c], preferred_element_type=F32)
        return carry

    lax.fori_loop(0, FFN_NC, chunk, 0)
    o_ref[...] = x + 0.5 * acc_ref[...]


def _ffn(x, g, wgu, wo):
    T = x.shape[0]
    tm = min(FFN_TM, T)
    return pl.pallas_call(
        _ffn_kernel,
        grid=(T // tm,),
        in_specs=[
            pl.BlockSpec((tm, D_MODEL), lambda i: (i, 0)),
            _resident((1, D_MODEL)),
            _resident(wgu.shape),
            _resident(wo.shape),
        ],
        out_specs=pl.BlockSpec((tm, D_MODEL), lambda i: (i, 0)),
        out_shape=jax.ShapeDtypeStruct((T, D_MODEL), F32),
        scratch_shapes=[pltpu.VMEM((tm, D_MODEL), BF16), pltpu.VMEM((tm, D_MODEL), F32)],
        compiler_params=pltpu.CompilerParams(
            dimension_semantics=("arbitrary",), vmem_limit_bytes=VMEM_LIMIT),
        name="ffn",
    )(x, g, wgu, wo)


def _lru_kernel(reverse, n_tiles, tl, *refs):
    if reverse:
        (xp_ref, xc_ref, xn_ref, g_ref, win_ref, bin_ref, cw_ref, cb_ref, gw_ref, gb_ref, lam_ref,
         o_ref, ubuf, abuf, bbuf, carry_ref) = refs
    else:
        (xp_ref, xc_ref, xn_ref, g_ref, win_ref, bin_ref, cw_ref, cb_ref, gw_ref, gb_ref, lam_ref,
         hb_ref, wout_ref, bout_ref,
         o_ref, ubuf, abuf, bbuf, carry_ref, ybuf) = refs
    L = LRU_WIDTH
    halo = SUBLANES
    i = pl.program_id(1)
    t = (n_tiles - 1 - i) if reverse else i

    @pl.when(i == 0)
    def _():
        carry_ref[...] = jnp.zeros_like(carry_ref)

    g = g_ref[...]
    x = xc_ref[0]
    hn = _rmsnorm(x, g).astype(BF16)
    b_in = bin_ref[...]
    if reverse:
        u_pre = jnp.dot(hn, win_ref[:, L:], preferred_element_type=F32) + b_in[:, L:]
    else:
        yu = jnp.dot(hn, win_ref[...], preferred_element_type=F32) + b_in
        ybuf[...] = jax.nn.gelu(yu[:, :L])
        u_pre = yu[:, L:]
    ubuf[halo:halo + tl, :] = u_pre

    xh = jnp.concatenate([xp_ref[0], xn_ref[0]], axis=0)
    uh = jnp.dot(_rmsnorm(xh, g).astype(BF16), win_ref[:, L:], preferred_element_type=F32) + b_in[:, L:]
    ubuf[0:halo, :] = jnp.where(t > 0, uh[:halo], 0.0)
    ubuf[halo + tl:, :] = jnp.where(t < n_tiles - 1, uh[halo:], 0.0)

    cw = cw_ref[...]
    u = cb_ref[...] + sum(
        ubuf[halo - CONV_LEFT + k: halo - CONV_LEFT + k + tl, :] * cw[k:k + 1, :] for k in range(CONV_W))

    lam = lam_ref[...]
    neg_c_sp = -LRU_C * jax.nn.softplus(-lam)
    gb = gb_ref[...]
    W = LRU_BLOCK_W
    for n in range(LRU_BLOCKS):
        sl = slice(n * W, (n + 1) * W)
        un = u[:, sl]
        gg = jnp.dot(un.astype(BF16), gw_ref[n], preferred_element_type=F32)
        r = jax.nn.sigmoid(gg[:, :W] + gb[0:1, sl])
        ig = jax.nn.sigmoid(gg[:, W:] + gb[1:2, sl])
        a = jnp.exp(r * neg_c_sp[:, sl])
        abuf[:, sl] = a
        bbuf[:, sl] = jnp.sqrt(1.0 - a * a) * ig * un

    row = lax.broadcasted_iota(jnp.int32, (SUBLANES, L), 0)
    steps = (1, 2, 4)
    if reverse:
        masks = [row < SUBLANES - s for s in steps]
        shifts = [SUBLANES - s for s in steps]
    else:
        masks = [row >= s for s in steps]
        shifts = list(steps)
    n_groups = tl // SUBLANES

    def group(gidx, carry):
        gi = (n_groups - 1 - gidx) if reverse else gidx
        r0 = pl.multiple_of(gi * SUBLANES, SUBLANES)
        A = abuf[pl.ds(r0, SUBLANES), :]
        B = bbuf[pl.ds(r0, SUBLANES), :]
        for m, sh in zip(masks, shifts):
            A_sh = jnp.where(m, pltpu.roll(A, sh, 0), 1.0)
            B_sh = jnp.where(m, pltpu.roll(B, sh, 0), 0.0)
            B = A * B_sh + B
            A = A * A_sh
        h = A * carry + B
        bbuf[pl.ds(r0, SUBLANES), :] = h
        edge = h[0:1, :] if reverse else h[SUBLANES - 1:SUBLANES, :]
        return jnp.broadcast_to(edge, (SUBLANES, L))

    carry_ref[...] = lax.fori_loop(0, n_groups, group, carry_ref[...])

    if reverse:
        o_ref[0] = bbuf[...]
    else:
        z = ((bbuf[...] + hb_ref[0]) * ybuf[...]).astype(BF16)
        o_ref[0] = x + jnp.dot(z, wout_ref[...], preferred_element_type=F32) + bout_ref[...]


def _lru_pass(x, reverse, g, w_in, b_in, conv_w, conv_b, gw, gb, lam, hb=None, w_out=None, b_out=None):
    B, S, D = x.shape
    L = LRU_WIDTH
    tl = min(LRU_TL, S)
    nt = S // tl
    hb_per_tile = tl // SUBLANES
    n_hblocks = S // SUBLANES

    def tile(i):
        return (nt - 1 - i) if reverse else i

    x_specs = [
        pl.BlockSpec((1, SUBLANES, D), lambda b, i: (b, jnp.maximum(tile(i) * hb_per_tile - 1, 0), 0)),
        pl.BlockSpec((1, tl, D), lambda b, i: (b, tile(i), 0)),
        pl.BlockSpec((1, SUBLANES, D),
                     lambda b, i: (b, jnp.minimum((tile(i) + 1) * hb_per_tile, n_hblocks - 1), 0)),
    ]
    common = [g, w_in, b_in, conv_w, conv_b, gw, gb, lam]
    in_specs = x_specs + [_resident(a.shape) for a in common]
    args = [x, x, x] + common
    scratch = [pltpu.VMEM((tl + 2 * SUBLANES, L), F32), pltpu.VMEM((tl, L), F32),
               pltpu.VMEM((tl, L), F32), pltpu.VMEM((SUBLANES, L), F32)]
    if not reverse:
        in_specs += [pl.BlockSpec((1, tl, L), lambda b, i: (b, i, 0)),
                     _resident(w_out.shape), _resident(b_out.shape)]
        args += [hb, w_out, b_out]
        scratch += [pltpu.VMEM((tl, L), F32)]
    return pl.pallas_call(
        functools.partial(_lru_kernel, reverse, nt, tl),
        grid=(B, nt),
        in_specs=in_specs,
        out_specs=pl.BlockSpec((1, tl, D), lambda b, i: (b, tile(i), 0)),
        out_shape=jax.ShapeDtypeStruct((B, S, D), F32),
        scratch_shapes=scratch,
        compiler_params=pltpu.CompilerParams(
            dimension_semantics=("arbitrary", "arbitrary"), vmem_limit_bytes=VMEM_LIMIT),
        name="lru_bwd" if reverse else "lru_fwd",
    )(*args)


def _qkv_kernel(tm, x_ref, g_ref, w_ref, qg_ref, kg_ref, cos_ref, sin_ref, o0_ref, o1_ref, o2_ref,
                hn_ref, dbuf):
    hn_ref[...] = _rmsnorm(x_ref[0], g_ref[...]).astype(BF16)
    cos = cos_ref[...]
    sin = sin_ref[...]
    width = N_HEADS * HEAD_DIM
    o_refs = (o0_ref, o1_ref, o2_ref)
    for gi, (_, dil) in enumerate(GROUPS):
        o_ref = o_refs[gi]
        for which in range(3):
            c0 = (gi * 3 + which) * width
            t = jnp.dot(hn_ref[...], w_ref[:, c0:c0 + width], preferred_element_type=F32)
            gain = (qg_ref, kg_ref, None)[which]
            post = HEAD_DIM ** -0.5 if which == 0 else 1.0
            for h in range(N_HEADS):
                th = t[:, h * HEAD_DIM:(h + 1) * HEAD_DIM]
                if gain is not None:
                    inv = lax.rsqrt(jnp.mean(th * th, axis=-1, keepdims=True) + EPS) * post
                    tn = th * inv * gain[gi:gi + 1, :]
                    th = tn * cos + pltpu.roll(tn, HEAD_DIM // 2, 1) * sin
                dst = slice(which * width + h * HEAD_DIM, which * width + (h + 1) * HEAD_DIM)
                if dil == 1:
                    o_ref[0, :, dst] = th.astype(BF16)
                else:
                    dbuf[h] = th
                    for r in range(dil):
                        o_ref[r, :, dst] = dbuf[h, pl.ds(r, tm // dil, stride=dil), :].astype(BF16)


def _qkv(x, g, w, q_gain, k_gain, cos, sin):
    B, S, D = x.shape
    tm = min(QKV_TM, S)
    width = N_HEADS * HEAD_DIM
    return pl.pallas_call(
        functools.partial(_qkv_kernel, tm),
        grid=(B, S // tm),
        in_specs=[
            pl.BlockSpec((1, tm, D), lambda b, i: (b, i, 0)),
            _resident(g.shape), _resident(w.shape), _resident(q_gain.shape), _resident(k_gain.shape),
            pl.BlockSpec((tm, HEAD_DIM), lambda b, i: (i, 0)),
            pl.BlockSpec((tm, HEAD_DIM), lambda b, i: (i, 0)),
        ],
        out_specs=[pl.BlockSpec((None, dil, tm // dil, 3 * width), lambda b, i: (b, 0, i, 0))
                   for _, dil in GROUPS],
        out_shape=[jax.ShapeDtypeStruct((B, dil, S // dil, 3 * width), BF16) for _, dil in GROUPS],
        scratch_shapes=[pltpu.VMEM((tm, D), BF16), pltpu.VMEM((N_HEADS, tm, HEAD_DIM), F32)],
        compiler_params=pltpu.CompilerParams(
            dimension_semantics=("arbitrary", "arbitrary"), vmem_limit_bytes=VMEM_LIMIT),
        name="qkv",
    )(x, g, w, q_gain, k_gain, cos, sin)


def _attn_kernel(tl, sub_len, q_ref, kp_ref, kc_ref, kn_ref, vp_ref, vc_ref, vn_ref,
                 o_ref, st_ref, kbuf, vbuf):
    R = ATTN_REACH
    j = pl.program_id(2)
    kbuf[0:R, :] = kp_ref[...]
    kbuf[R:R + tl, :] = kc_ref[...]
    kbuf[R + tl:, :] = kn_ref[...]
    vbuf[0:R, :] = vp_ref[...]
    vbuf[R:R + tl, :] = vc_ref[...]
    vbuf[R + tl:, :] = vn_ref[...]

    nk = tl + 2 * R
    row = lax.broadcasted_iota(jnp.int32, (tl, nk), 0)
    col = lax.broadcasted_iota(jnp.int32, (tl, nk), 1)
    rel = col - row
    kpos = col + (j * tl - R)
    bias = jnp.where(jnp.abs(rel - R) <= R, 0.0, NEG_BIG)
    bias = jnp.where(kpos >= 0, bias, NEG_BIG)
    bias = jnp.where(kpos < sub_len, bias, NEG_BIG)

    lane = lax.broadcasted_iota(jnp.int32, (tl, LANES), 1)
    stats = jnp.zeros((tl, LANES), F32)
    for h in range(N_HEADS):
        sl = slice(h * HEAD_DIM, (h + 1) * HEAD_DIM)
        s = lax.dot_general(q_ref[:, sl], kbuf[:, sl], (((1,), (1,)), ((), ())),
                            preferred_element_type=F32)
        s = s + bias
        m = jnp.max(s, axis=-1, keepdims=True)
        p = jnp.exp(s - m)
        l = jnp.sum(p, axis=-1, keepdims=True)
        o = jnp.dot(p.astype(BF16), vbuf[:, sl], preferred_element_type=F32)
        o_ref[:, sl] = o / l
        stats = jnp.where(lane == h, m + jnp.log(l), stats)
    st_ref[...] = stats


def _attn_group(qkv):
    B, dil, sub_len, _ = qkv.shape
    width = N_HEADS * HEAD_DIM
    tl = min(ATTN_TL, sub_len)
    R = ATTN_REACH
    per = tl // R
    n_rblocks = sub_len // R

    def cur(c):
        return pl.BlockSpec((None, None, tl, width), lambda b, r, j: (b, r, j, c))

    def prev(c):
        return pl.BlockSpec((None, None, R, width),
                            lambda b, r, j: (b, r, jnp.maximum(j * per - 1, 0), c))

    def nxt(c):
        return pl.BlockSpec((None, None, R, width),
                            lambda b, r, j: (b, r, jnp.minimum((j + 1) * per, n_rblocks - 1), c))

    return pl.pallas_call(
        functools.partial(_attn_kernel, tl, sub_len),
        grid=(B, dil, sub_len // tl),
        in_specs=[cur(0), prev(1), cur(1), nxt(1), prev(2), cur(2), nxt(2)],
        out_specs=[pl.BlockSpec((None, None, tl, width), lambda b, r, j: (b, r, j, 0)),
                   pl.BlockSpec((None, None, tl, LANES), lambda b, r, j: (b, r, j, 0))],
        out_shape=[jax.ShapeDtypeStruct((B, dil, sub_len, width), F32),
                   jax.ShapeDtypeStruct((B, dil, sub_len, LANES), F32)],
        scratch_shapes=[pltpu.VMEM((tl + 2 * R, width), BF16), pltpu.VMEM((tl + 2 * R, width), BF16)],
        compiler_params=pltpu.CompilerParams(
            dimension_semantics=("arbitrary", "arbitrary", "arbitrary"), vmem_limit_bytes=VMEM_LIMIT),
        name=f"attn_d{dil}",
    )(qkv, qkv, qkv, qkv, qkv, qkv, qkv)


def _merge_kernel(tm, x_ref, o0_ref, o1_ref, o2_ref, s0_ref, s1_ref, s2_ref, wo_ref, out_ref,
                  mbuf, obuf, sbuf):
    for gi, (o_ref, s_ref) in enumerate(((o1_ref, s1_ref), (o2_ref, s2_ref))):
        dil = GROUPS[gi + 1][1]
        for r in range(dil):
            rows = pl.ds(r, tm // dil, stride=dil)
            sbuf[gi, rows, :] = s_ref[r]
            for h in range(N_HEADS):
                obuf[gi, h, rows, :] = o_ref[r, :, h * HEAD_DIM:(h + 1) * HEAD_DIM]
    l0, l1, l2 = s0_ref[0], sbuf[0], sbuf[1]
    m = jnp.maximum(jnp.maximum(l0, l1), l2)
    e0, e1, e2 = jnp.exp(l0 - m), jnp.exp(l1 - m), jnp.exp(l2 - m)
    inv = 1.0 / (e0 + e1 + e2)
    w0, w1, w2 = e0 * inv, e1 * inv, e2 * inv
    for h in range(N_HEADS):
        sl = slice(h * HEAD_DIM, (h + 1) * HEAD_DIM)
        oh = (w0[:, h:h + 1] * o0_ref[0, :, sl] + w1[:, h:h + 1] * obuf[0, h]
              + w2[:, h:h + 1] * obuf[1, h])
        mbuf[:, sl] = oh.astype(BF16)
    out_ref[0] = x_ref[0] + jnp.dot(mbuf[...], wo_ref[...], preferred_element_type=F32)


def _merge(x, outs, stats, w_o):
    B, S, D = x.shape
    tm = min(MERGE_TM, S)
    tok = lambda w: pl.BlockSpec((1, tm, w), lambda b, i: (b, i, 0))
    grp = lambda dil, w: pl.BlockSpec((None, dil, tm // dil, w), lambda b, i: (b, 0, i, 0))
    dils = [dil for _, dil in GROUPS]
    return pl.pallas_call(
        functools.partial(_merge_kernel, tm),
        grid=(B, S // tm),
        in_specs=([tok(D)] + [grp(dil, D) for dil in dils] + [grp(dil, LANES) for dil in dils]
                  + [_resident(w_o.shape)]),
        out_specs=tok(D),
        out_shape=jax.ShapeDtypeStruct((B, S, D), F32),
        scratch_shapes=[pltpu.VMEM((tm, D), BF16),
                        pltpu.VMEM((N_GROUPS - 1, N_HEADS, tm, HEAD_DIM), F32),
                        pltpu.VMEM((N_GROUPS - 1, tm, LANES), F32)],
        compiler_params=pltpu.CompilerParams(
            dimension_semantics=("arbitrary", "arbitrary"), vmem_limit_bytes=VMEM_LIMIT),
        name="attn_merge",
    )(x, *outs, *stats, w_o)


def _rope_tables(S):
    half = HEAD_DIM // 2
    inv = ROPE_THETA ** (-2.0 * jnp.arange(half, dtype=F32) / HEAD_DIM)
    ang = jnp.arange(S, dtype=F32)[:, None] * inv[None, :]
    cos, sin = jnp.cos(ang), jnp.sin(ang)
    return jnp.concatenate([cos, cos], axis=-1), jnp.concatenate([-sin, sin], axis=-1)


def _prep_ffn(w_in, w_out):
    gate = w_in[:, :D_FF].reshape(D_MODEL, FFN_NC, FFN_TF)
    up = w_in[:, D_FF:].reshape(D_MODEL, FFN_NC, FFN_TF)
    wgu = jnp.concatenate([gate, up], axis=-1).transpose(1, 0, 2).astype(BF16)
    return wgu, w_out.reshape(FFN_NC, FFN_TF, D_MODEL).astype(BF16)


def _prep_gates(gate_w, gate_b, d):
    gw = jnp.concatenate([gate_w[d, 0], gate_w[d, 1]], axis=-1).astype(BF16)
    return gw, gate_b[d].reshape(2, LRU_WIDTH)


def _trunk(x, p):
    B, S, D = x.shape
    row = lambda v: v.reshape(1, -1)

    def ffn(x, i, k):
        return _ffn(x.reshape(B * S, D), row(p["norm_g"][i, k * 2]), *p["ffn"][i][k]).reshape(B, S, D)

    x = ffn(x, 0, 0)
    lru_common = (row(p["norm_g"][0, 1]), p["lru_w_in"], row(p["lru_b_in"]), p["lru_conv_w"],
                  row(p["lru_conv_b"]))
    hb = _lru_pass(x, True, *lru_common, *p["lru_gates"][1], row(p["lru_lambda"][1]))
    x = _lru_pass(x, False, *lru_common, *p["lru_gates"][0], row(p["lru_lambda"][0]),
                  hb=hb, w_out=p["lru_w_out"], b_out=row(p["lru_b_out"]))
    x = ffn(x, 0, 1)
    x = ffn(x, 1, 0)
    cos, sin = _rope_tables(S)
    qkv = _qkv(x, row(p["norm_g"][1, 1]), p["attn_w_qkv"], p["attn_q_gain"], p["attn_k_gain"], cos, sin)
    outs, stats = zip(*[_attn_group(t) for t in qkv])
    x = _merge(x, outs, stats, p["attn_w_o"])
    x = ffn(x, 1, 1)
    return x


def kernel(x_prompt, x_sample, norm_g, ffn_w_in, ffn_w_out, lru_w_in, lru_b_in, lru_conv_w, lru_conv_b, lru_gate_w, lru_gate_b, lru_lambda, lru_w_out, lru_b_out, attn_w_qkv, attn_q_gain, attn_k_gain, attn_w_o):
    p = {
        "norm_g": norm_g,
        "ffn": [[_prep_ffn(ffn_w_in[i, k], ffn_w_out[i, k]) for k in range(2)] for i in range(2)],
        "lru_w_in": lru_w_in[0].astype(BF16),
        "lru_b_in": lru_b_in[0],
        "lru_conv_w": lru_conv_w[0],
        "lru_conv_b": lru_conv_b[0],
        "lru_gates": [_prep_gates(lru_gate_w[0], lru_gate_b[0], d) for d in range(2)],
        "lru_lambda": lru_lambda[0],
        "lru_w_out": lru_w_out[0].astype(BF16),
        "lru_b_out": lru_b_out[0],
        "attn_w_qkv": attn_w_qkv[0].astype(BF16),
        "attn_q_gain": attn_q_gain[0],
        "attn_k_gain": attn_k_gain[0],
        "attn_w_o": attn_w_o[0].astype(BF16),
    }
    return _trunk(x_prompt, p), _trunk(x_sample, p)
```

```python
import functools

import jax
import jax.numpy as jnp
from jax import lax
from jax.experimental import pallas as pl
from jax.experimental.pallas import tpu as pltpu

F32 = jnp.float32
BF16 = jnp.bfloat16

D_MODEL = 1024
D_FF = 2816
LRU_WIDTH = 1024
LRU_BLOCKS = 8
LRU_BLOCK_W = LRU_WIDTH // LRU_BLOCKS
CONV_W = 4
CONV_LEFT = 2
LRU_C = 8.0
HEAD_DIM = 128
N_HEADS = D_MODEL // HEAD_DIM
GROUPS = ((128, 1), (512, 4), (2048, 16))
N_GROUPS = len(GROUPS)
ROPE_THETA = 10000.0
EPS = 1e-6

SUBLANES = 8
LANES = 128
MXU_DIM = 256
VMEM_LIMIT = 56 * 1024 * 1024

FFN_TM = 512
FFN_TF = MXU_DIM
FFN_NC = D_FF // FFN_TF
LRU_TL = 512
QKV_TM = 256
ATTN_REACH = 64
ATTN_TL = 512
MERGE_TM = 512
NEG_BIG = -1e30


def _resident(shape):
    nd = len(shape)
    return pl.BlockSpec(shape, lambda *_: (0,) * nd, pipeline_mode=pl.Buffered(1))


def _rmsnorm(x, g):
    return x * lax.rsqrt(jnp.mean(x * x, axis=-1, keepdims=True) + EPS) * g


def _ffn_kernel(x_ref, g_ref, win_ref, wo_ref, o_ref, xn_ref, acc_ref):
    xn_ref[...] = _rmsnorm(x_ref[...], g_ref[...]).astype(BF16)
    for c in range(FFN_NC):
        cols = slice(c * FFN_TF, (c + 1) * FFN_TF)
        gate = jnp.dot(xn_ref[...], win_ref[:, cols], preferred_element_type=F32)
        up = jnp.dot(xn_ref[...], win_ref[:, D_FF + c * FFN_TF: D_FF + (c + 1) * FFN_TF],
                     preferred_element_type=F32)
        h = (gate * jax.nn.sigmoid(gate) * up).astype(BF16)
        part = jnp.dot(h, wo_ref[cols, :], preferred_element_type=F32)
        if c == 0:
            acc_ref[...] = part
        elif c < FFN_NC - 1:
            acc_ref[...] += part
        else:
            o_ref[...] = x_ref[...] + 0.5 * (acc_ref[...] + part)


def _ffn(x, g, w_in, wo):
    T = x.shape[0]
    tm = min(FFN_TM, T)
    return pl.pallas_call(
        _ffn_kernel,
        grid=(T // tm,),
        in_specs=[
            pl.BlockSpec((tm, D_MODEL), lambda i: (i, 0)),
            _resident((1, D_MODEL)),
            _resident(w_in.shape),
            _resident(wo.shape),
        ],
        out_specs=pl.BlockSpec((tm, D_MODEL), lambda i: (i, 0)),
        out_shape=jax.ShapeDtypeStruct((T, D_MODEL), F32),
        scratch_shapes=[pltpu.VMEM((tm, D_MODEL), BF16), pltpu.VMEM((tm, D_MODEL), F32)],
        compiler_params=pltpu.CompilerParams(
            dimension_semantics=("arbitrary",), vmem_limit_bytes=VMEM_LIMIT),
        name="ffn",
    )(x, g, w_in, wo)


def _lru_kernel(reverse, n_tiles, tl, *refs):
    if reverse:
        (xp_ref, xc_ref, xn_ref, g_ref, win_ref, bin_ref, cw_ref, cb_ref, gw_ref, gb_ref, lam_ref,
         o_ref, ubuf, abuf, bbuf, carry_ref) = refs
    else:
        (xp_ref, xc_ref, xn_ref, g_ref, win_ref, bin_ref, cw_ref, cb_ref, gw_ref, gb_ref, lam_ref,
         hb_ref, wout_ref, bout_ref,
         o_ref, ubuf, abuf, bbuf, carry_ref, ybuf) = refs
    L = LRU_WIDTH
    halo = SUBLANES
    i = pl.program_id(1)
    t = (n_tiles - 1 - i) if reverse else i

    @pl.when(i == 0)
    def _():
        carry_ref[...] = jnp.zeros_like(carry_ref)

    g = g_ref[...]
    x = xc_ref[0]
    hn = _rmsnorm(x, g).astype(BF16)
    b_in = bin_ref[...]
    if reverse:
        u_pre = jnp.dot(hn, win_ref[:, L:], preferred_element_type=F32) + b_in[:, L:]
    else:
        yu = jnp.dot(hn, win_ref[...], preferred_element_type=F32) + b_in
        ybuf[...] = jax.nn.gelu(yu[:, :L])
        u_pre = yu[:, L:]
    ubuf[halo:halo + tl, :] = u_pre

    xh = jnp.concatenate([xp_ref[0], xn_ref[0]], axis=0)
    uh = jnp.dot(_rmsnorm(xh, g).astype(BF16), win_ref[:, L:], preferred_element_type=F32) + b_in[:, L:]
    ubuf[0:halo, :] = jnp.where(t > 0, uh[:halo], 0.0)
    ubuf[halo + tl:, :] = jnp.where(t < n_tiles - 1, uh[halo:], 0.0)

    cw = cw_ref[...]
    u = cb_ref[...] + sum(
        ubuf[halo - CONV_LEFT + k: halo - CONV_LEFT + k + tl, :] * cw[k:k + 1, :] for k in range(CONV_W))

    lam = lam_ref[...]
    neg_c_sp = -LRU_C * jax.nn.softplus(-lam)
    gb = gb_ref[...]
    W = LRU_BLOCK_W
    for n in range(LRU_BLOCKS):
        sl = slice(n * W, (n + 1) * W)
        un = u[:, sl]
        gg = jnp.dot(un.astype(BF16), gw_ref[n], preferred_element_type=F32)
        r = jax.nn.sigmoid(gg[:, :W] + gb[0:1, sl])
        ig = jax.nn.sigmoid(gg[:, W:] + gb[1:2, sl])
        a = jnp.exp(r * neg_c_sp[:, sl])
        abuf[:, sl] = a
        bbuf[:, sl] = jnp.sqrt(1.0 - a * a) * ig * un

    row = lax.broadcasted_iota(jnp.int32, (SUBLANES, L), 0)
    steps = (1, 2, 4)
    if reverse:
        masks = [row < SUBLANES - s for s in steps]
        shifts = [SUBLANES - s for s in steps]
    else:
        masks = [row >= s for s in steps]
        shifts = list(steps)
    n_groups = tl // SUBLANES

    def group(gidx, carry):
        gi = (n_groups - 1 - gidx) if reverse else gidx
        r0 = pl.multiple_of(gi * SUBLANES, SUBLANES)
        A = abuf[pl.ds(r0, SUBLANES), :]
        B = bbuf[pl.ds(r0, SUBLANES), :]
        for m, sh in zip(masks, shifts):
            A_sh = jnp.where(m, pltpu.roll(A, sh, 0), 1.0)
            B_sh = jnp.where(m, pltpu.roll(B, sh, 0), 0.0)
            B = A * B_sh + B
            A = A * A_sh
        h = A * carry + B
        bbuf[pl.ds(r0, SUBLANES), :] = h
        edge = h[0:1, :] if reverse else h[SUBLANES - 1:SUBLANES, :]
        return jnp.broadcast_to(edge, (SUBLANES, L))

    carry_ref[...] = lax.fori_loop(0, n_groups, group, carry_ref[...])

    if reverse:
        o_ref[0] = bbuf[...]
    else:
        z = ((bbuf[...] + hb_ref[0]) * ybuf[...]).astype(BF16)
        o_ref[0] = x + jnp.dot(z, wout_ref[...], preferred_element_type=F32) + bout_ref[...]


def _lru_pass(x, reverse, g, w_in, b_in, conv_w, conv_b, gw, gb, lam, hb=None, w_out=None, b_out=None):
    B, S, D = x.shape
    L = LRU_WIDTH
    tl = min(LRU_TL, S)
    nt = S // tl
    hb_per_tile = tl // SUBLANES
    n_hblocks = S // SUBLANES

    def tile(i):
        return (nt - 1 - i) if reverse else i

    x_specs = [
        pl.BlockSpec((1, SUBLANES, D), lambda b, i: (b, jnp.maximum(tile(i) * hb_per_tile - 1, 0), 0)),
        pl.BlockSpec((1, tl, D), lambda b, i: (b, tile(i), 0)),
        pl.BlockSpec((1, SUBLANES, D),
                     lambda b, i: (b, jnp.minimum((tile(i) + 1) * hb_per_tile, n_hblocks - 1), 0)),
    ]
    common = [g, w_in, b_in, conv_w, conv_b, gw, gb, lam]
    in_specs = x_specs + [_resident(a.shape) for a in common]
    args = [x, x, x] + common
    scratch = [pltpu.VMEM((tl + 2 * SUBLANES, L), F32), pltpu.VMEM((tl, L), F32),
               pltpu.VMEM((tl, L), F32), pltpu.VMEM((SUBLANES, L), F32)]
    if not reverse:
        in_specs += [pl.BlockSpec((1, tl, L), lambda b, i: (b, i, 0)),
                     _resident(w_out.shape), _resident(b_out.shape)]
        args += [hb, w_out, b_out]
        scratch += [pltpu.VMEM((tl, L), F32)]
    return pl.pallas_call(
        functools.partial(_lru_kernel, reverse, nt, tl),
        grid=(B, nt),
        in_specs=in_specs,
        out_specs=pl.BlockSpec((1, tl, D), lambda b, i: (b, tile(i), 0)),
        out_shape=jax.ShapeDtypeStruct((B, S, D), F32),
        scratch_shapes=scratch,
        compiler_params=pltpu.CompilerParams(
            dimension_semantics=("arbitrary", "arbitrary"), vmem_limit_bytes=VMEM_LIMIT),
        name="lru_bwd" if reverse else "lru_fwd",
    )(*args)


def _qkv_kernel(tm, x_ref, g_ref, w_ref, qg_ref, kg_ref, cos_ref, sin_ref, o0_ref, o1_ref, o2_ref,
                hn_ref, dbuf):
    hn_ref[...] = _rmsnorm(x_ref[0], g_ref[...]).astype(BF16)
    cos = cos_ref[...]
    sin = sin_ref[...]
    width = N_HEADS * HEAD_DIM
    o_refs = (o0_ref, o1_ref, o2_ref)
    for gi, (_, dil) in enumerate(GROUPS):
        o_ref = o_refs[gi]
        for which in range(3):
            c0 = (gi * 3 + which) * width
            t = jnp.dot(hn_ref[...], w_ref[:, c0:c0 + width], preferred_element_type=F32)
            gain = (qg_ref, kg_ref, None)[which]
            post = HEAD_DIM ** -0.5 if which == 0 else 1.0
            for h in range(N_HEADS):
                th = t[:, h * HEAD_DIM:(h + 1) * HEAD_DIM]
                if gain is not None:
                    inv = lax.rsqrt(jnp.mean(th * th, axis=-1, keepdims=True) + EPS) * post
                    tn = th * inv * gain[gi:gi + 1, :]
                    th = tn * cos + pltpu.roll(tn, HEAD_DIM // 2, 1) * sin
                dst = slice(which * width + h * HEAD_DIM, which * width + (h + 1) * HEAD_DIM)
                if dil == 1:
                    o_ref[0, :, dst] = th.astype(BF16)
                else:
                    dbuf[h] = th
                    for r in range(dil):
                        o_ref[r, :, dst] = dbuf[h, pl.ds(r, tm // dil, stride=dil), :].astype(BF16)


def _qkv(x, g, w, q_gain, k_gain, cos, sin):
    B, S, D = x.shape
    tm = min(QKV_TM, S)
    width = N_HEADS * HEAD_DIM
    return pl.pallas_call(
        functools.partial(_qkv_kernel, tm),
        grid=(B, S // tm),
        in_specs=[
            pl.BlockSpec((1, tm, D), lambda b, i: (b, i, 0)),
            _resident(g.shape), _resident(w.shape), _resident(q_gain.shape), _resident(k_gain.shape),
            pl.BlockSpec((tm, HEAD_DIM), lambda b, i: (i, 0)),
            pl.BlockSpec((tm, HEAD_DIM), lambda b, i: (i, 0)),
        ],
        out_specs=[pl.BlockSpec((None, dil, tm // dil, 3 * width), lambda b, i: (b, 0, i, 0))
                   for _, dil in GROUPS],
        out_shape=[jax.ShapeDtypeStruct((B, dil, S // dil, 3 * width), BF16) for _, dil in GROUPS],
        scratch_shapes=[pltpu.VMEM((tm, D), BF16), pltpu.VMEM((N_HEADS, tm, HEAD_DIM), F32)],
        compiler_params=pltpu.CompilerParams(
            dimension_semantics=("arbitrary", "arbitrary"), vmem_limit_bytes=VMEM_LIMIT),
        name="qkv",
    )(x, g, w, q_gain, k_gain, cos, sin)


def _attn_kernel(tl, sub_len, q_ref, kp_ref, kc_ref, kn_ref, vp_ref, vc_ref, vn_ref,
                 o_ref, st_ref, kbuf, vbuf):
    R = ATTN_REACH
    j = pl.program_id(2)
    kbuf[0:R, :] = kp_ref[...]
    kbuf[R:R + tl, :] = kc_ref[...]
    kbuf[R + tl:, :] = kn_ref[...]
    vbuf[0:R, :] = vp_ref[...]
    vbuf[R:R + tl, :] = vc_ref[...]
    vbuf[R + tl:, :] = vn_ref[...]

    qb = 2 * R
    nk = qb + 2 * R
    row = lax.broadcasted_iota(jnp.int32, (qb, nk), 0)
    col = lax.broadcasted_iota(jnp.int32, (qb, nk), 1)
    band = jnp.where(jnp.abs(col - row - R) <= R, 0.0, NEG_BIG)
    lane = lax.broadcasted_iota(jnp.int32, (qb, LANES), 1)
    ones = jnp.ones((nk, HEAD_DIM), BF16)
    n_sub = tl // qb
    for sb in range(n_sub):
        rows = slice(sb * qb, (sb + 1) * qb)
        keys = slice(sb * qb, sb * qb + nk)
        bias = band
        if sb == 0:
            bias = jnp.where(col + (j * tl - R) >= 0, bias, NEG_BIG)
        if sb == n_sub - 1:
            bias = jnp.where(col + (j * tl + sb * qb - R) < sub_len, bias, NEG_BIG)
        stats = jnp.zeros((qb, LANES), F32)
        for h in range(N_HEADS):
            sl = slice(h * HEAD_DIM, (h + 1) * HEAD_DIM)
            s = lax.dot_general(q_ref[rows, sl], kbuf[keys, sl], (((1,), (1,)), ((), ())),
                                preferred_element_type=F32) + bias
            m = jnp.max(s, axis=-1, keepdims=True)
            p = jnp.exp(s - m).astype(BF16)
            v1 = jnp.concatenate([vbuf[keys, sl], ones], axis=1)
            ol = jnp.dot(p, v1, preferred_element_type=F32)
            l = ol[:, HEAD_DIM:]
            o_ref[rows, sl] = ol[:, :HEAD_DIM] / l
            stats = jnp.where(lane == h, m + jnp.log(l), stats)
        st_ref[rows, :] = stats


def _attn_group(qkv):
    B, dil, sub_len, _ = qkv.shape
    width = N_HEADS * HEAD_DIM
    tl = min(ATTN_TL, sub_len)
    R = ATTN_REACH
    per = tl // R
    n_rblocks = sub_len // R

    def cur(c):
        return pl.BlockSpec((None, None, tl, width), lambda b, r, j: (b, r, j, c))

    def prev(c):
        return pl.BlockSpec((None, None, R, width),
                            lambda b, r, j: (b, r, jnp.maximum(j * per - 1, 0), c))

    def nxt(c):
        return pl.BlockSpec((None, None, R, width),
                            lambda b, r, j: (b, r, jnp.minimum((j + 1) * per, n_rblocks - 1), c))

    return pl.pallas_call(
        functools.partial(_attn_kernel, tl, sub_len),
        grid=(B, dil, sub_len // tl),
        in_specs=[cur(0), prev(1), cur(1), nxt(1), prev(2), cur(2), nxt(2)],
        out_specs=[pl.BlockSpec((None, None, tl, width), lambda b, r, j: (b, r, j, 0)),
                   pl.BlockSpec((None, None, tl, LANES), lambda b, r, j: (b, r, j, 0))],
        out_shape=[jax.ShapeDtypeStruct((B, dil, sub_len, width), F32),
                   jax.ShapeDtypeStruct((B, dil, sub_len, LANES), F32)],
        scratch_shapes=[pltpu.VMEM((tl + 2 * R, width), BF16), pltpu.VMEM((tl + 2 * R, width), BF16)],
        compiler_params=pltpu.CompilerParams(
            dimension_semantics=("arbitrary", "arbitrary", "arbitrary"), vmem_limit_bytes=VMEM_LIMIT),
        name=f"attn_d{dil}",
    )(qkv, qkv, qkv, qkv, qkv, qkv, qkv)


def _merge_kernel(tm, x_ref, o0_ref, o1_ref, o2_ref, s0_ref, s1_ref, s2_ref, wo_ref, out_ref,
                  mbuf, obuf, sbuf):
    for gi, (o_ref, s_ref) in enumerate(((o1_ref, s1_ref), (o2_ref, s2_ref))):
        dil = GROUPS[gi + 1][1]
        for r in range(dil):
            rows = pl.ds(r, tm // dil, stride=dil)
            sbuf[gi, rows, :] = s_ref[r]
            for h in range(N_HEADS):
                obuf[gi, h, rows, :] = o_ref[r, :, h * HEAD_DIM:(h + 1) * HEAD_DIM]
    l0, l1, l2 = s0_ref[0], sbuf[0], sbuf[1]
    m = jnp.maximum(jnp.maximum(l0, l1), l2)
    e0, e1, e2 = jnp.exp(l0 - m), jnp.exp(l1 - m), jnp.exp(l2 - m)
    inv = 1.0 / (e0 + e1 + e2)
    w0, w1, w2 = e0 * inv, e1 * inv, e2 * inv
    for h in range(N_HEADS):
        sl = slice(h * HEAD_DIM, (h + 1) * HEAD_DIM)
        oh = (w0[:, h:h + 1] * o0_ref[0, :, sl] + w1[:, h:h + 1] * obuf[0, h]
              + w2[:, h:h + 1] * obuf[1, h])
        mbuf[:, sl] = oh.astype(BF16)
    out_ref[0] = x_ref[0] + jnp.dot(mbuf[...], wo_ref[...], preferred_element_type=F32)


def _merge(x, outs, stats, w_o):
    B, S, D = x.shape
    tm = min(MERGE_TM, S)
    tok = lambda w: pl.BlockSpec((1, tm, w), lambda b, i: (b, i, 0))
    grp = lambda dil, w: pl.BlockSpec((None, dil, tm // dil, w), lambda b, i: (b, 0, i, 0))
    dils = [dil for _, dil in GROUPS]
    return pl.pallas_call(
        functools.partial(_merge_kernel, tm),
        grid=(B, S // tm),
        in_specs=([tok(D)] + [grp(dil, D) for dil in dils] + [grp(dil, LANES) for dil in dils]
                  + [_resident(w_o.shape)]),
        out_specs=tok(D),
        out_shape=jax.ShapeDtypeStruct((B, S, D), F32),
        scratch_shapes=[pltpu.VMEM((tm, D), BF16),
                        pltpu.VMEM((N_GROUPS - 1, N_HEADS, tm, HEAD_DIM), F32),
                        pltpu.VMEM((N_GROUPS - 1, tm, LANES), F32)],
        compiler_params=pltpu.CompilerParams(
            dimension_semantics=("arbitrary", "arbitrary"), vmem_limit_bytes=VMEM_LIMIT),
        name="attn_merge",
    )(x, *outs, *stats, w_o)


def _rope_tables(S):
    half = HEAD_DIM // 2
    inv = ROPE_THETA ** (-2.0 * jnp.arange(half, dtype=F32) / HEAD_DIM)
    ang = jnp.arange(S, dtype=F32)[:, None] * inv[None, :]
    cos, sin = jnp.cos(ang), jnp.sin(ang)
    return jnp.concatenate([cos, cos], axis=-1), jnp.concatenate([-sin, sin], axis=-1)


def _prep_gates(gate_w, gate_b, d):
    gw = jnp.concatenate([gate_w[d, 0], gate_w[d, 1]], axis=-1).astype(BF16)
    return gw, gate_b[d].reshape(2, LRU_WIDTH)


def _trunk(x, p):
    B, S, D = x.shape
    row = lambda v: v.reshape(1, -1)

    def ffn(x, i, k):
        return _ffn(x.reshape(B * S, D), row(p["norm_g"][i, k * 2]), *p["ffn"][i][k]).reshape(B, S, D)

    x = ffn(x, 0, 0)
    lru_common = (row(p["norm_g"][0, 1]), p["lru_w_in"], row(p["lru_b_in"]), p["lru_conv_w"],
                  row(p["lru_conv_b"]))
    hb = _lru_pass(x, True, *lru_common, *p["lru_gates"][1], row(p["lru_lambda"][1]))
    x = _lru_pass(x, False, *lru_common, *p["lru_gates"][0], row(p["lru_lambda"][0]),
                  hb=hb, w_out=p["lru_w_out"], b_out=row(p["lru_b_out"]))
    x = ffn(x, 0, 1)
    x = ffn(x, 1, 0)
    qkv = _qkv(x, row(p["norm_g"][1, 1]), p["attn_w_qkv"], p["attn_q_gain"], p["attn_k_gain"], *p["rope"])
    outs, stats = zip(*[_attn_group(t) for t in qkv])
    x = _merge(x, outs, stats, p["attn_w_o"])
    x = ffn(x, 1, 1)
    return x


def kernel(x_prompt, x_sample, norm_g, ffn_w_in, ffn_w_out, lru_w_in, lru_b_in, lru_conv_w, lru_conv_b, lru_gate_w, lru_gate_b, lru_lambda, lru_w_out, lru_b_out, attn_w_qkv, attn_q_gain, attn_k_gain, attn_w_o):
    p = {
        "norm_g": norm_g,
        "ffn": [[(ffn_w_in[i, k].astype(BF16), ffn_w_out[i, k].astype(BF16)) for k in range(2)]
                for i in range(2)],
        "lru_w_in": lru_w_in[0].astype(BF16),
        "lru_b_in": lru_b_in[0],
        "lru_conv_w": lru_conv_w[0],
        "lru_conv_b": lru_conv_b[0],
        "lru_gates": [_prep_gates(lru_gate_w[0], lru_gate_b[0], d) for d in range(2)],
        "lru_lambda": lru_lambda[0],
        "lru_w_out": lru_w_out[0].astype(BF16),
        "lru_b_out": lru_b_out[0],
        "attn_w_qkv": attn_w_qkv[0].astype(BF16),
        "attn_q_gain": attn_q_gain[0],
        "attn_k_gain": attn_k_gain[0],
        "attn_w_o": attn_w_o[0].astype(BF16),
        "rope": _rope_tables(max(x_prompt.shape[1], x_sample.shape[1])),
    }
    return _trunk(x_prompt, p), _trunk(x_sample, p)
```

```python
import functools

import jax
import jax.numpy as jnp
from jax import lax
from jax.experimental import pallas as pl
from jax.experimental.pallas import tpu as pltpu

F32 = jnp.float32
BF16 = jnp.bfloat16

D_MODEL = 1024
D_FF = 2816
LRU_WIDTH = 1024
LRU_BLOCKS = 8
LRU_BLOCK_W = LRU_WIDTH // LRU_BLOCKS
CONV_W = 4
CONV_LEFT = 2
LRU_C = 8.0
HEAD_DIM = 128
N_HEADS = D_MODEL // HEAD_DIM
GROUPS = ((128, 1), (512, 4), (2048, 16))
N_GROUPS = len(GROUPS)
ROPE_THETA = 10000.0
EPS = 1e-6

SUBLANES = 8
LANES = 128
MXU_DIM = 256
VMEM_LIMIT = 56 * 1024 * 1024

N_SLABS = D_MODEL // LANES
FFN_TM = 512
FFN_TF = MXU_DIM
FFN_NC = D_FF // FFN_TF
LRU_TL = FFN_TM
QKV_TM = 256
ATTN_REACH = 64
ATTN_TL = 512
MERGE_TM = 512
NEG_BIG = -1e30


def _resident(shape):
    nd = len(shape)
    return pl.BlockSpec(shape, lambda *_: (0,) * nd, pipeline_mode=pl.Buffered(1))


def _rmsnorm(x, g):
    return x * lax.rsqrt(jnp.mean(x * x, axis=-1, keepdims=True) + EPS) * g


def _from_slabs(ref):
    return jnp.concatenate([ref[n] for n in range(N_SLABS)], axis=1)


def _ffn_kernel(interleave, x_ref, g_ref, win_ref, wo_ref, o_ref, xn_ref, acc_ref):
    xn_ref[...] = _rmsnorm(x_ref[...], g_ref[...]).astype(BF16)
    for c in range(FFN_NC):
        cols = slice(c * FFN_TF, (c + 1) * FFN_TF)
        gate = jnp.dot(xn_ref[...], win_ref[:, cols], preferred_element_type=F32)
        up = jnp.dot(xn_ref[...], win_ref[:, D_FF + c * FFN_TF: D_FF + (c + 1) * FFN_TF],
                     preferred_element_type=F32)
        h = (gate * jax.nn.sigmoid(gate) * up).astype(BF16)
        part = jnp.dot(h, wo_ref[cols, :], preferred_element_type=F32)
        if c == 0:
            acc_ref[...] = part
        elif c < FFN_NC - 1:
            acc_ref[...] += part
        elif not interleave:
            o_ref[...] = x_ref[...] + 0.5 * (acc_ref[...] + part)
        else:
            acc_ref[...] = x_ref[...] + 0.5 * (acc_ref[...] + part)
    if interleave:
        chunk = acc_ref.shape[0] // SUBLANES
        for n in range(N_SLABS):
            for j in range(SUBLANES):
                o_ref[n, pl.ds(j, chunk, stride=SUBLANES), :] = (
                    acc_ref[j * chunk:(j + 1) * chunk, n * LANES:(n + 1) * LANES])


def _ffn(x, g, w_in, wo, interleave=False):
    B, S, D = x.shape
    T = B * S
    tm = min(FFN_TM, S)
    nts = S // tm
    if interleave:
        out_spec = pl.BlockSpec((None, N_SLABS, tm, LANES), lambda i: (i // nts, 0, i % nts, 0))
        out_shape = jax.ShapeDtypeStruct((B, N_SLABS, S, LANES), F32)
    else:
        out_spec = pl.BlockSpec((tm, D), lambda i: (i, 0))
        out_shape = jax.ShapeDtypeStruct((T, D), F32)
    out = pl.pallas_call(
        functools.partial(_ffn_kernel, interleave),
        grid=(T // tm,),
        in_specs=[
            pl.BlockSpec((tm, D), lambda i: (i, 0)),
            _resident((1, D)),
            _resident(w_in.shape),
            _resident(wo.shape),
        ],
        out_specs=out_spec,
        out_shape=out_shape,
        scratch_shapes=[pltpu.VMEM((tm, D), BF16), pltpu.VMEM((tm, D), F32)],
        compiler_params=pltpu.CompilerParams(
            dimension_semantics=("arbitrary",), vmem_limit_bytes=VMEM_LIMIT),
        name="ffn",
    )(x.reshape(T, D), g, w_in, wo)
    return out if interleave else out.reshape(B, S, D)


def _lru_kernel(reverse, n_tiles, tl, *refs):
    if reverse:
        (xp_ref, xc_ref, xn_ref, g_ref, win_ref, bin_ref, cw_ref, cb_ref, gw_ref, gb_ref, lam_ref,
         o_ref, ubuf, abuf, bbuf, carry_ref) = refs
    else:
        (xp_ref, xc_ref, xn_ref, g_ref, win_ref, bin_ref, cw_ref, cb_ref, gw_ref, gb_ref, lam_ref,
         hb_ref, wout_ref, bout_ref,
         o_ref, ubuf, abuf, bbuf, carry_ref, ybuf, pbuf) = refs
    L = LRU_WIDTH
    S8 = SUBLANES
    n_off = tl // S8
    i = pl.program_id(1)
    t = (n_tiles - 1 - i) if reverse else i

    @pl.when(i == 0)
    def _():
        carry_ref[...] = jnp.zeros_like(carry_ref)

    g = g_ref[...]
    x = _from_slabs(xc_ref)
    hn = _rmsnorm(x, g).astype(BF16)
    b_in = bin_ref[...]
    if reverse:
        u_pre = jnp.dot(hn, win_ref[:, L:], preferred_element_type=F32) + b_in[:, L:]
    else:
        yu = jnp.dot(hn, win_ref[...], preferred_element_type=F32) + b_in
        ybuf[...] = jax.nn.gelu(yu[:, :L])
        u_pre = yu[:, L:]

    xh = jnp.concatenate([_from_slabs(xp_ref), _from_slabs(xn_ref)], axis=0)
    uh = jnp.dot(_rmsnorm(xh, g).astype(BF16), win_ref[:, L:], preferred_element_type=F32) + b_in[:, L:]
    u_m2 = jnp.where(t > 0, uh[S8 - 1:S8], 0.0)
    u_m1 = jnp.where(t > 0, uh[2 * S8 - 1:2 * S8], 0.0)
    u_p1 = jnp.where(t < n_tiles - 1, uh[2 * S8:2 * S8 + 1], 0.0)

    sub = lax.broadcasted_iota(jnp.int32, (S8, L), 0)
    ubuf[0:S8, :] = jnp.where(sub == 0, u_m2, pltpu.roll(u_pre[tl - 2 * S8:tl - S8], 1, 0))
    ubuf[S8:2 * S8, :] = jnp.where(sub == 0, u_m1, pltpu.roll(u_pre[tl - S8:tl], 1, 0))
    ubuf[2 * S8:2 * S8 + tl, :] = u_pre
    ubuf[2 * S8 + tl:, :] = jnp.where(sub == S8 - 1, u_p1, pltpu.roll(u_pre[0:S8], S8 - 1, 0))
    cw = cw_ref[...]
    u = cb_ref[...] + sum(ubuf[k * S8:k * S8 + tl, :] * cw[k:k + 1, :] for k in range(CONV_W))

    lam = lam_ref[...]
    half_log_a1 = (-0.5 * LRU_C) * jax.nn.softplus(-lam)
    gb = gb_ref[...]
    W = LRU_BLOCK_W
    for n in range(LRU_BLOCKS):
        sl = slice(n * W, (n + 1) * W)
        un = u[:, sl]
        gg = jnp.dot(un.astype(BF16), gw_ref[n], preferred_element_type=F32)
        th_r = jnp.tanh(gg[:, :W] + gb[0:1, sl])
        th_i = jnp.tanh(gg[:, W:] + gb[1:2, sl])
        k = half_log_a1[:, sl]
        a = jnp.exp(k + k * th_r)
        abuf[:, sl] = a
        bbuf[:, sl] = jnp.sqrt(1.0 - a * a) * ((0.5 * un) * (1.0 + th_i))

    def offset_step(k, carry):
        h, p = carry
        gi = (n_off - 1 - k) if reverse else k
        r0 = pl.multiple_of(gi * S8, S8)
        a = abuf[pl.ds(r0, S8), :]
        h = a * h + bbuf[pl.ds(r0, S8), :]
        p = a * p
        bbuf[pl.ds(r0, S8), :] = h
        abuf[pl.ds(r0, S8), :] = p
        return h, p

    hh, pp = lax.fori_loop(0, n_off, offset_step, (jnp.zeros((S8, L), F32), jnp.ones((S8, L), F32)),
                           unroll=4)
    for s in (1, 2, 4):
        if reverse:
            m, sh = sub < S8 - s, S8 - s
        else:
            m, sh = sub >= s, s
        h_sh = jnp.where(m, pltpu.roll(hh, sh, 0), 0.0)
        p_sh = jnp.where(m, pltpu.roll(pp, sh, 0), 1.0)
        hh = pp * h_sh + hh
        pp = pp * p_sh
    c_in = carry_ref[...]
    e = pp * c_in + hh
    if reverse:
        c = jnp.where(sub == S8 - 1, c_in, pltpu.roll(e, S8 - 1, 0))
        carry_ref[...] = jnp.broadcast_to(e[0:1, :], (S8, L))
    else:
        c = jnp.where(sub == 0, c_in, pltpu.roll(e, 1, 0))
        carry_ref[...] = jnp.broadcast_to(e[S8 - 1:S8, :], (S8, L))
    h = (bbuf[...].reshape(n_off, S8, L) + abuf[...].reshape(n_off, S8, L) * c[None]).reshape(tl, L)

    if reverse:
        o_ref[0] = h
    else:
        z = ((h + hb_ref[0]) * ybuf[...]).astype(BF16)
        res = x + jnp.dot(z, wout_ref[...], preferred_element_type=F32) + bout_ref[...]
        for n in range(N_SLABS):
            pbuf[n] = res[:, n * LANES:(n + 1) * LANES]
        for n in range(N_SLABS):
            for j in range(S8):
                o_ref[0, j * n_off:(j + 1) * n_off, n * LANES:(n + 1) * LANES] = (
                    pbuf[n, pl.ds(j, n_off, stride=S8), :])


def _lru_pass(x, reverse, g, w_in, b_in, conv_w, conv_b, gw, gb, lam, hb=None, w_out=None, b_out=None):
    B, _, S, _ = x.shape
    L = LRU_WIDTH
    D = D_MODEL
    tl = min(LRU_TL, S)
    nt = S // tl
    S8 = SUBLANES

    def tile(i):
        return (nt - 1 - i) if reverse else i

    x_specs = [
        pl.BlockSpec((None, N_SLABS, 2 * S8, LANES),
                     lambda b, i: (b, 0, jnp.maximum(tile(i) * (tl // (2 * S8)) - 1, 0), 0)),
        pl.BlockSpec((None, N_SLABS, tl, LANES), lambda b, i: (b, 0, tile(i), 0)),
        pl.BlockSpec((None, N_SLABS, S8, LANES),
                     lambda b, i: (b, 0, jnp.minimum((tile(i) + 1) * (tl // S8), S // S8 - 1), 0)),
    ]
    common = [g, w_in, b_in, conv_w, conv_b, gw, gb, lam]
    in_specs = x_specs + [_resident(a.shape) for a in common]
    args = [x, x, x] + common
    scratch = [pltpu.VMEM((tl + 3 * S8, L), F32), pltpu.VMEM((tl, L), F32),
               pltpu.VMEM((tl, L), F32), pltpu.VMEM((S8, L), F32)]
    if not reverse:
        in_specs += [pl.BlockSpec((1, tl, L), lambda b, i: (b, i, 0)),
                     _resident(w_out.shape), _resident(b_out.shape)]
        args += [hb, w_out, b_out]
        scratch += [pltpu.VMEM((tl, L), F32), pltpu.VMEM((N_SLABS, tl, LANES), F32)]
    return pl.pallas_call(
        functools.partial(_lru_kernel, reverse, nt, tl),
        grid=(B, nt),
        in_specs=in_specs,
        out_specs=pl.BlockSpec((1, tl, D), lambda b, i: (b, tile(i), 0)),
        out_shape=jax.ShapeDtypeStruct((B, S, D), F32),
        scratch_shapes=scratch,
        compiler_params=pltpu.CompilerParams(
            dimension_semantics=("arbitrary", "arbitrary"), vmem_limit_bytes=VMEM_LIMIT),
        name="lru_bwd" if reverse else "lru_fwd",
    )(*args)


def _qkv_kernel(tm, x_ref, g_ref, w_ref, qg_ref, kg_ref, cos_ref, sin_ref, o0_ref, o1_ref, o2_ref,
                hn_ref, dbuf):
    hn_ref[...] = _rmsnorm(x_ref[0], g_ref[...]).astype(BF16)
    cos = cos_ref[...]
    sin = sin_ref[...]
    width = N_HEADS * HEAD_DIM
    o_refs = (o0_ref, o1_ref, o2_ref)
    for gi, (_, dil) in enumerate(GROUPS):
        o_ref = o_refs[gi]
        for which in range(3):
            c0 = (gi * 3 + which) * width
            t = jnp.dot(hn_ref[...], w_ref[:, c0:c0 + width], preferred_element_type=F32)
            gain = (qg_ref, kg_ref, None)[which]
            post = HEAD_DIM ** -0.5 if which == 0 else 1.0
            for h in range(N_HEADS):
                th = t[:, h * HEAD_DIM:(h + 1) * HEAD_DIM]
                if gain is not None:
                    inv = lax.rsqrt(jnp.mean(th * th, axis=-1, keepdims=True) + EPS) * post
                    tn = th * inv * gain[gi:gi + 1, :]
                    th = tn * cos + pltpu.roll(tn, HEAD_DIM // 2, 1) * sin
                dst = slice(which * width + h * HEAD_DIM, which * width + (h + 1) * HEAD_DIM)
                if dil == 1:
                    o_ref[0, :, dst] = th.astype(BF16)
                else:
                    dbuf[h] = th
                    for r in range(dil):
                        o_ref[r, :, dst] = dbuf[h, pl.ds(r, tm // dil, stride=dil), :].astype(BF16)


def _qkv(x, g, w, q_gain, k_gain, cos, sin):
    B, S, D = x.shape
    tm = min(QKV_TM, S)
    width = N_HEADS * HEAD_DIM
    return pl.pallas_call(
        functools.partial(_qkv_kernel, tm),
        grid=(B, S // tm),
        in_specs=[
            pl.BlockSpec((1, tm, D), lambda b, i: (b, i, 0)),
            _resident(g.shape), _resident(w.shape), _resident(q_gain.shape), _resident(k_gain.shape),
            pl.BlockSpec((tm, HEAD_DIM), lambda b, i: (i, 0)),
            pl.BlockSpec((tm, HEAD_DIM), lambda b, i: (i, 0)),
        ],
        out_specs=[pl.BlockSpec((None, dil, tm // dil, 3 * width), lambda b, i: (b, 0, i, 0))
                   for _, dil in GROUPS],
        out_shape=[jax.ShapeDtypeStruct((B, dil, S // dil, 3 * width), BF16) for _, dil in GROUPS],
        scratch_shapes=[pltpu.VMEM((tm, D), BF16), pltpu.VMEM((N_HEADS, tm, HEAD_DIM), F32)],
        compiler_params=pltpu.CompilerParams(
            dimension_semantics=("arbitrary", "arbitrary"), vmem_limit_bytes=VMEM_LIMIT),
        name="qkv",
    )(x, g, w, q_gain, k_gain, cos, sin)


def _attn_kernel(n_res, tl, sub_len, q_ref, kp_ref, kc_ref, kn_ref, vp_ref, vc_ref, vn_ref,
                 o_ref, st_ref, kbuf, vbuf):
    R = ATTN_REACH
    j = pl.program_id(2)
    qb = 2 * R
    nk = qb + 2 * R
    row = lax.broadcasted_iota(jnp.int32, (qb, nk), 0)
    col = lax.broadcasted_iota(jnp.int32, (qb, nk), 1)
    band = jnp.where(jnp.abs(col - row - R) <= R, 0.0, NEG_BIG)
    lane = lax.broadcasted_iota(jnp.int32, (qb, LANES), 1)
    ones = jnp.ones((nk, HEAD_DIM), BF16)
    n_sub = tl // qb
    for rr in range(n_res):
        kbuf[rr, 0:R, :] = kp_ref[rr]
        kbuf[rr, R:R + tl, :] = kc_ref[rr]
        kbuf[rr, R + tl:, :] = kn_ref[rr]
        vbuf[rr, 0:R, :] = vp_ref[rr]
        vbuf[rr, R:R + tl, :] = vc_ref[rr]
        vbuf[rr, R + tl:, :] = vn_ref[rr]
        for sb in range(n_sub):
            rows = slice(sb * qb, (sb + 1) * qb)
            keys = slice(sb * qb, sb * qb + nk)
            bias = band
            if sb == 0:
                bias = jnp.where(col + (j * tl - R) >= 0, bias, NEG_BIG)
            if sb == n_sub - 1:
                bias = jnp.where(col + (j * tl + sb * qb - R) < sub_len, bias, NEG_BIG)
            stats = jnp.zeros((qb, LANES), F32)
            for h in range(N_HEADS):
                sl = slice(h * HEAD_DIM, (h + 1) * HEAD_DIM)
                s = lax.dot_general(q_ref[rr, rows, sl], kbuf[rr, keys, sl], (((1,), (1,)), ((), ())),
                                    preferred_element_type=F32) + bias
                m = jnp.max(s, axis=-1, keepdims=True)
                p = jnp.exp(s - m).astype(BF16)
                v1 = jnp.concatenate([vbuf[rr, keys, sl], ones], axis=1)
                ol = jnp.dot(p, v1, preferred_element_type=F32)
                l = ol[:, HEAD_DIM:]
                o_ref[rr, rows, sl] = ol[:, :HEAD_DIM] / l
                stats = jnp.where(lane == h, m + jnp.log(l), stats)
            st_ref[rr, rows, :] = stats


def _attn_group(qkv):
    B, dil, sub_len, _ = qkv.shape
    width = N_HEADS * HEAD_DIM
    tl = min(ATTN_TL, sub_len)
    n_res = min(dil, ATTN_TL // tl)
    R = ATTN_REACH
    per = tl // R
    n_rblocks = sub_len // R

    def cur(c, w=width):
        return pl.BlockSpec((None, n_res, tl, w), lambda b, r, j: (b, r, j, c))

    def prev(c):
        return pl.BlockSpec((None, n_res, R, width),
                            lambda b, r, j: (b, r, jnp.maximum(j * per - 1, 0), c))

    def nxt(c):
        return pl.BlockSpec((None, n_res, R, width),
                            lambda b, r, j: (b, r, jnp.minimum((j + 1) * per, n_rblocks - 1), c))

    return pl.pallas_call(
        functools.partial(_attn_kernel, n_res, tl, sub_len),
        grid=(B, dil // n_res, sub_len // tl),
        in_specs=[cur(0), prev(1), cur(1), nxt(1), prev(2), cur(2), nxt(2)],
        out_specs=[cur(0), cur(0, LANES)],
        out_shape=[jax.ShapeDtypeStruct((B, dil, sub_len, width), F32),
                   jax.ShapeDtypeStruct((B, dil, sub_len, LANES), F32)],
        scratch_shapes=[pltpu.VMEM((n_res, tl + 2 * R, width), BF16),
                        pltpu.VMEM((n_res, tl + 2 * R, width), BF16)],
        compiler_params=pltpu.CompilerParams(
            dimension_semantics=("arbitrary", "arbitrary", "arbitrary"), vmem_limit_bytes=VMEM_LIMIT),
        name=f"attn_d{dil}",
    )(qkv, qkv, qkv, qkv, qkv, qkv, qkv)


def _merge_kernel(tm, x_ref, o0_ref, o1_ref, o2_ref, s0_ref, s1_ref, s2_ref, wo_ref, out_ref,
                  mbuf, obuf, sbuf):
    for gi, (o_ref, s_ref) in enumerate(((o1_ref, s1_ref), (o2_ref, s2_ref))):
        dil = GROUPS[gi + 1][1]
        for r in range(dil):
            rows = pl.ds(r, tm // dil, stride=dil)
            sbuf[gi, rows, :] = s_ref[r]
            for h in range(N_HEADS):
                obuf[gi, h, rows, :] = o_ref[r, :, h * HEAD_DIM:(h + 1) * HEAD_DIM]
    l0, l1, l2 = s0_ref[0], sbuf[0], sbuf[1]
    m = jnp.maximum(jnp.maximum(l0, l1), l2)
    e0, e1, e2 = jnp.exp(l0 - m), jnp.exp(l1 - m), jnp.exp(l2 - m)
    inv = 1.0 / (e0 + e1 + e2)
    w0, w1, w2 = e0 * inv, e1 * inv, e2 * inv
    for h in range(N_HEADS):
        sl = slice(h * HEAD_DIM, (h + 1) * HEAD_DIM)
        oh = (w0[:, h:h + 1] * o0_ref[0, :, sl] + w1[:, h:h + 1] * obuf[0, h]
              + w2[:, h:h + 1] * obuf[1, h])
        mbuf[:, sl] = oh.astype(BF16)
    out_ref[0] = x_ref[0] + jnp.dot(mbuf[...], wo_ref[...], preferred_element_type=F32)


def _merge(x, outs, stats, w_o):
    B, S, D = x.shape
    tm = min(MERGE_TM, S)
    tok = lambda w: pl.BlockSpec((1, tm, w), lambda b, i: (b, i, 0))
    grp = lambda dil, w: pl.BlockSpec((None, dil, tm // dil, w), lambda b, i: (b, 0, i, 0))
    dils = [dil for _, dil in GROUPS]
    return pl.pallas_call(
        functools.partial(_merge_kernel, tm),
        grid=(B, S // tm),
        in_specs=([tok(D)] + [grp(dil, D) for dil in dils] + [grp(dil, LANES) for dil in dils]
                  + [_resident(w_o.shape)]),
        out_specs=tok(D),
        out_shape=jax.ShapeDtypeStruct((B, S, D), F32),
        scratch_shapes=[pltpu.VMEM((tm, D), BF16),
                        pltpu.VMEM((N_GROUPS - 1, N_HEADS, tm, HEAD_DIM), F32),
                        pltpu.VMEM((N_GROUPS - 1, tm, LANES), F32)],
        compiler_params=pltpu.CompilerParams(
            dimension_semantics=("arbitrary", "arbitrary"), vmem_limit_bytes=VMEM_LIMIT),
        name="attn_merge",
    )(x, *outs, *stats, w_o)


def _rope_tables(S):
    half = HEAD_DIM // 2
    inv = ROPE_THETA ** (-2.0 * jnp.arange(half, dtype=F32) / HEAD_DIM)
    ang = jnp.arange(S, dtype=F32)[:, None] * inv[None, :]
    cos, sin = jnp.cos(ang), jnp.sin(ang)
    return jnp.concatenate([cos, cos], axis=-1), jnp.concatenate([-sin, sin], axis=-1)


def _prep_gates(gate_w, gate_b, d):
    gw = (0.5 * jnp.concatenate([gate_w[d, 0], gate_w[d, 1]], axis=-1)).astype(BF16)
    return gw, 0.5 * gate_b[d].reshape(2, LRU_WIDTH)


def _trunk(x, p):
    row = lambda v: v.reshape(1, -1)

    def ffn(x, i, k, **kw):
        return _ffn(x, row(p["norm_g"][i, k * 2]), *p["ffn"][i][k], **kw)

    xs = ffn(x, 0, 0, interleave=True)
    lru_common = (row(p["norm_g"][0, 1]), p["lru_w_in"], row(p["lru_b_in"]), p["lru_conv_w"],
                  row(p["lru_conv_b"]))
    hb = _lru_pass(xs, True, *lru_common, *p["lru_gates"][1], row(p["lru_lambda"][1]))
    x = _lru_pass(xs, False, *lru_common, *p["lru_gates"][0], row(p["lru_lambda"][0]),
                  hb=hb, w_out=p["lru_w_out"], b_out=row(p["lru_b_out"]))
    x = ffn(x, 0, 1)
    x = ffn(x, 1, 0)
    qkv = _qkv(x, row(p["norm_g"][1, 1]), p["attn_w_qkv"], p["attn_q_gain"], p["attn_k_gain"], *p["rope"])
    outs, stats = zip(*[_attn_group(t) for t in qkv])
    x = _merge(x, outs, stats, p["attn_w_o"])
    x = ffn(x, 1, 1)
    return x


def kernel(x_prompt, x_sample, norm_g, ffn_w_in, ffn_w_out, lru_w_in, lru_b_in, lru_conv_w, lru_conv_b, lru_gate_w, lru_gate_b, lru_lambda, lru_w_out, lru_b_out, attn_w_qkv, attn_q_gain, attn_k_gain, attn_w_o):
    p = {
        "norm_g": norm_g,
        "ffn": [[(ffn_w_in[i, k].astype(BF16), ffn_w_out[i, k].astype(BF16)) for k in range(2)]
                for i in range(2)],
        "lru_w_in": lru_w_in[0].astype(BF16),
        "lru_b_in": lru_b_in[0],
        "lru_conv_w": lru_conv_w[0],
        "lru_conv_b": lru_conv_b[0],
        "lru_gates": [_prep_gates(lru_gate_w[0], lru_gate_b[0], d) for d in range(2)],
        "lru_lambda": lru_lambda[0],
        "lru_w_out": lru_w_out[0].astype(BF16),
        "lru_b_out": lru_b_out[0],
        "attn_w_qkv": attn_w_qkv[0].astype(BF16),
        "attn_q_gain": attn_q_gain[0],
        "attn_k_gain": attn_k_gain[0],
        "attn_w_o": attn_w_o[0].astype(BF16),
        "rope": _rope_tables(max(x_prompt.shape[1], x_sample.shape[1])),
    }
    return _trunk(x_prompt, p), _trunk(x_sample, p)
```

```python
import functools

import jax
import jax.numpy as jnp
from jax import lax
from jax.experimental import pallas as pl
from jax.experimental.pallas import tpu as pltpu

F32 = jnp.float32
BF16 = jnp.bfloat16

D_MODEL = 1024
D_FF = 2816
LRU_WIDTH = 1024
LRU_BLOCKS = 8
LRU_BLOCK_W = LRU_WIDTH // LRU_BLOCKS
CONV_W = 4
CONV_LEFT = 2
LRU_C = 8.0
HEAD_DIM = 128
N_HEADS = D_MODEL // HEAD_DIM
GROUPS = ((128, 1), (512, 4), (2048, 16))
N_GROUPS = len(GROUPS)
ROPE_THETA = 10000.0
EPS = 1e-6

SUBLANES = 8
LANES = 128
MXU_DIM = 256
VMEM_LIMIT = 56 * 1024 * 1024

N_SLABS = D_MODEL // LANES
FFN_TM = 512
FFN_TF = MXU_DIM
FFN_NC = D_FF // FFN_TF
LRU_TL = FFN_TM
QKV_TM = 256
ATTN_REACH = 64
ATTN_TL = 512
MERGE_TM = 512
NEG_BIG = -1e30


def _resident(shape):
    nd = len(shape)
    return pl.BlockSpec(shape, lambda *_: (0,) * nd, pipeline_mode=pl.Buffered(1))


def _rmsnorm(x, g):
    return x * lax.rsqrt(jnp.mean(x * x, axis=-1, keepdims=True) + EPS) * g


def _from_slabs(ref):
    return jnp.concatenate([ref[n] for n in range(N_SLABS)], axis=1)


def _ffn_layer(x_ref, g_ref, win_ref, wo_ref, xn_ref, acc_ref, dst_ref):
    xn_ref[...] = _rmsnorm(x_ref[...], g_ref[...]).astype(BF16)
    for c in range(FFN_NC):
        cols = slice(c * FFN_TF, (c + 1) * FFN_TF)
        gate = jnp.dot(xn_ref[...], win_ref[:, cols], preferred_element_type=F32)
        up = jnp.dot(xn_ref[...], win_ref[:, D_FF + c * FFN_TF: D_FF + (c + 1) * FFN_TF],
                     preferred_element_type=F32)
        h = (gate * jax.nn.sigmoid(gate) * up).astype(BF16)
        part = jnp.dot(h, wo_ref[cols, :], preferred_element_type=F32)
        if c == 0:
            acc_ref[...] = part
        elif c < FFN_NC - 1:
            acc_ref[...] += part
        else:
            dst_ref[...] = x_ref[...] + 0.5 * (acc_ref[...] + part)


def _ffn_kernel(n_layers, interleave, x_ref, *refs):
    params = [refs[3 * k:3 * k + 3] for k in range(n_layers)]
    o_ref, xn_ref, acc_ref = refs[3 * n_layers:3 * n_layers + 3]
    mid_refs = refs[3 * n_layers + 3:]
    src = x_ref
    for k, (g_ref, win_ref, wo_ref) in enumerate(params):
        last = k == n_layers - 1
        dst = (acc_ref if interleave else o_ref) if last else mid_refs[k]
        _ffn_layer(src, g_ref, win_ref, wo_ref, xn_ref, acc_ref, dst)
        src = dst
    if interleave:
        chunk = acc_ref.shape[0] // SUBLANES
        for n in range(N_SLABS):
            for j in range(SUBLANES):
                o_ref[n, pl.ds(j, chunk, stride=SUBLANES), :] = (
                    acc_ref[j * chunk:(j + 1) * chunk, n * LANES:(n + 1) * LANES])


def _ffn(x, layers, interleave=False):
    B, S, D = x.shape
    T = B * S
    tm = min(FFN_TM, S)
    nts = S // tm
    if interleave:
        out_spec = pl.BlockSpec((None, N_SLABS, tm, LANES), lambda i: (i // nts, 0, i % nts, 0))
        out_shape = jax.ShapeDtypeStruct((B, N_SLABS, S, LANES), F32)
    else:
        out_spec = pl.BlockSpec((tm, D), lambda i: (i, 0))
        out_shape = jax.ShapeDtypeStruct((T, D), F32)
    flat = [a for layer in layers for a in layer]
    out = pl.pallas_call(
        functools.partial(_ffn_kernel, len(layers), interleave),
        grid=(T // tm,),
        in_specs=[pl.BlockSpec((tm, D), lambda i: (i, 0))] + [_resident(a.shape) for a in flat],
        out_specs=out_spec,
        out_shape=out_shape,
        scratch_shapes=([pltpu.VMEM((tm, D), BF16), pltpu.VMEM((tm, D), F32)]
                        + [pltpu.VMEM((tm, D), F32)] * (len(layers) - 1)),
        compiler_params=pltpu.CompilerParams(
            dimension_semantics=("arbitrary",), vmem_limit_bytes=VMEM_LIMIT),
        name="ffn" if len(layers) == 1 else "ffn_pair",
    )(x.reshape(T, D), *flat)
    return out if interleave else out.reshape(B, S, D)


def _lru_kernel(reverse, n_tiles, tl, *refs):
    if reverse:
        (xp_ref, xc_ref, xn_ref, g_ref, win_ref, bin_ref, cw_ref, cb_ref, gw_ref, gb_ref, lam_ref,
         o_ref, ubuf, abuf, bbuf, carry_ref) = refs
    else:
        (xp_ref, xc_ref, xn_ref, g_ref, win_ref, bin_ref, cw_ref, cb_ref, gw_ref, gb_ref, lam_ref,
         hb_ref, wout_ref, bout_ref,
         o_ref, ubuf, abuf, bbuf, carry_ref, ybuf, pbuf) = refs
    L = LRU_WIDTH
    S8 = SUBLANES
    n_off = tl // S8
    i = pl.program_id(1)
    t = (n_tiles - 1 - i) if reverse else i

    @pl.when(i == 0)
    def _():
        carry_ref[...] = jnp.zeros_like(carry_ref)

    g = g_ref[...]
    x = _from_slabs(xc_ref)
    hn = _rmsnorm(x, g).astype(BF16)
    b_in = bin_ref[...]
    if reverse:
        u_pre = jnp.dot(hn, win_ref[:, L:], preferred_element_type=F32) + b_in[:, L:]
    else:
        yu = jnp.dot(hn, win_ref[...], preferred_element_type=F32) + b_in
        ybuf[...] = jax.nn.gelu(yu[:, :L])
        u_pre = yu[:, L:]

    xh = jnp.concatenate([_from_slabs(xp_ref), _from_slabs(xn_ref)], axis=0)
    uh = jnp.dot(_rmsnorm(xh, g).astype(BF16), win_ref[:, L:], preferred_element_type=F32) + b_in[:, L:]
    u_m2 = jnp.where(t > 0, uh[S8 - 1:S8], 0.0)
    u_m1 = jnp.where(t > 0, uh[2 * S8 - 1:2 * S8], 0.0)
    u_p1 = jnp.where(t < n_tiles - 1, uh[2 * S8:2 * S8 + 1], 0.0)

    sub = lax.broadcasted_iota(jnp.int32, (S8, L), 0)
    ubuf[0:S8, :] = jnp.where(sub == 0, u_m2, pltpu.roll(u_pre[tl - 2 * S8:tl - S8], 1, 0))
    ubuf[S8:2 * S8, :] = jnp.where(sub == 0, u_m1, pltpu.roll(u_pre[tl - S8:tl], 1, 0))
    ubuf[2 * S8:2 * S8 + tl, :] = u_pre
    ubuf[2 * S8 + tl:, :] = jnp.where(sub == S8 - 1, u_p1, pltpu.roll(u_pre[0:S8], S8 - 1, 0))
    cw = cw_ref[...]
    u = cb_ref[...] + sum(ubuf[k * S8:k * S8 + tl, :] * cw[k:k + 1, :] for k in range(CONV_W))

    lam = lam_ref[...]
    half_log_a1 = (-0.5 * LRU_C) * jax.nn.softplus(-lam)
    gb = gb_ref[...]
    W = LRU_BLOCK_W
    for n in range(LRU_BLOCKS):
        sl = slice(n * W, (n + 1) * W)
        un = u[:, sl]
        gg = jnp.dot(un.astype(BF16), gw_ref[n], preferred_element_type=F32)
        th_r = jnp.tanh(gg[:, :W] + gb[0:1, sl])
        th_i = jnp.tanh(gg[:, W:] + gb[1:2, sl])
        k = half_log_a1[:, sl]
        a = jnp.exp(k + k * th_r)
        abuf[:, sl] = a
        bbuf[:, sl] = jnp.sqrt(1.0 - a * a) * ((0.5 * un) * (1.0 + th_i))

    def offset_step(k, carry):
        h, p = carry
        gi = (n_off - 1 - k) if reverse else k
        r0 = pl.multiple_of(gi * S8, S8)
        a = abuf[pl.ds(r0, S8), :]
        h = a * h + bbuf[pl.ds(r0, S8), :]
        p = a * p
        bbuf[pl.ds(r0, S8), :] = h
        abuf[pl.ds(r0, S8), :] = p
        return h, p

    hh, pp = lax.fori_loop(0, n_off, offset_step, (jnp.zeros((S8, L), F32), jnp.ones((S8, L), F32)),
                           unroll=4)
    for s in (1, 2, 4):
        if reverse:
            m, sh = sub < S8 - s, S8 - s
        else:
            m, sh = sub >= s, s
        h_sh = jnp.where(m, pltpu.roll(hh, sh, 0), 0.0)
        p_sh = jnp.where(m, pltpu.roll(pp, sh, 0), 1.0)
        hh = pp * h_sh + hh
        pp = pp * p_sh
    c_in = carry_ref[...]
    e = pp * c_in + hh
    if reverse:
        c = jnp.where(sub == S8 - 1, c_in, pltpu.roll(e, S8 - 1, 0))
        carry_ref[...] = jnp.broadcast_to(e[0:1, :], (S8, L))
    else:
        c = jnp.where(sub == 0, c_in, pltpu.roll(e, 1, 0))
        carry_ref[...] = jnp.broadcast_to(e[S8 - 1:S8, :], (S8, L))
    h = (bbuf[...].reshape(n_off, S8, L) + abuf[...].reshape(n_off, S8, L) * c[None]).reshape(tl, L)

    if reverse:
        o_ref[0] = h
    else:
        z = ((h + hb_ref[0]) * ybuf[...]).astype(BF16)
        res = x + jnp.dot(z, wout_ref[...], preferred_element_type=F32) + bout_ref[...]
        for n in range(N_SLABS):
            pbuf[n] = res[:, n * LANES:(n + 1) * LANES]
        for n in range(N_SLABS):
            for j in range(S8):
                o_ref[0, j * n_off:(j + 1) * n_off, n * LANES:(n + 1) * LANES] = (
                    pbuf[n, pl.ds(j, n_off, stride=S8), :])


def _lru_pass(x, reverse, g, w_in, b_in, conv_w, conv_b, gw, gb, lam, hb=None, w_out=None, b_out=None):
    B, _, S, _ = x.shape
    L = LRU_WIDTH
    D = D_MODEL
    tl = min(LRU_TL, S)
    nt = S // tl
    S8 = SUBLANES

    def tile(i):
        return (nt - 1 - i) if reverse else i

    x_specs = [
        pl.BlockSpec((None, N_SLABS, 2 * S8, LANES),
                     lambda b, i: (b, 0, jnp.maximum(tile(i) * (tl // (2 * S8)) - 1, 0), 0)),
        pl.BlockSpec((None, N_SLABS, tl, LANES), lambda b, i: (b, 0, tile(i), 0)),
        pl.BlockSpec((None, N_SLABS, S8, LANES),
                     lambda b, i: (b, 0, jnp.minimum((tile(i) + 1) * (tl // S8), S // S8 - 1), 0)),
    ]
    common = [g, w_in, b_in, conv_w, conv_b, gw, gb, lam]
    in_specs = x_specs + [_resident(a.shape) for a in common]
    args = [x, x, x] + common
    scratch = [pltpu.VMEM((tl + 3 * S8, L), F32), pltpu.VMEM((tl, L), F32),
               pltpu.VMEM((tl, L), F32), pltpu.VMEM((S8, L), F32)]
    if not reverse:
        in_specs += [pl.BlockSpec((1, tl, L), lambda b, i: (b, i, 0)),
                     _resident(w_out.shape), _resident(b_out.shape)]
        args += [hb, w_out, b_out]
        scratch += [pltpu.VMEM((tl, L), F32), pltpu.VMEM((N_SLABS, tl, LANES), F32)]
    return pl.pallas_call(
        functools.partial(_lru_kernel, reverse, nt, tl),
        grid=(B, nt),
        in_specs=in_specs,
        out_specs=pl.BlockSpec((1, tl, D), lambda b, i: (b, tile(i), 0)),
        out_shape=jax.ShapeDtypeStruct((B, S, D), F32),
        scratch_shapes=scratch,
        compiler_params=pltpu.CompilerParams(
            dimension_semantics=("arbitrary", "arbitrary"), vmem_limit_bytes=VMEM_LIMIT),
        name="lru_bwd" if reverse else "lru_fwd",
    )(*args)


def _qkv_kernel(tm, x_ref, g_ref, w_ref, qg_ref, kg_ref, cos_ref, sin_ref, o0_ref, o1_ref, o2_ref,
                hn_ref, dbuf):
    hn_ref[...] = _rmsnorm(x_ref[0], g_ref[...]).astype(BF16)
    cos = cos_ref[...]
    sin = sin_ref[...]
    width = N_HEADS * HEAD_DIM
    o_refs = (o0_ref, o1_ref, o2_ref)
    for gi, (_, dil) in enumerate(GROUPS):
        o_ref = o_refs[gi]
        for which in range(3):
            c0 = (gi * 3 + which) * width
            t = jnp.dot(hn_ref[...], w_ref[:, c0:c0 + width], preferred_element_type=F32)
            gain = (qg_ref, kg_ref, None)[which]
            post = HEAD_DIM ** -0.5 if which == 0 else 1.0
            for h in range(N_HEADS):
                th = t[:, h * HEAD_DIM:(h + 1) * HEAD_DIM]
                if gain is not None:
                    inv = lax.rsqrt(jnp.mean(th * th, axis=-1, keepdims=True) + EPS) * post
                    tn = th * inv * gain[gi:gi + 1, :]
                    th = tn * cos + pltpu.roll(tn, HEAD_DIM // 2, 1) * sin
                dst = slice(which * width + h * HEAD_DIM, which * width + (h + 1) * HEAD_DIM)
                if dil == 1:
                    o_ref[0, :, dst] = th.astype(BF16)
                else:
                    dbuf[h] = th
                    for r in range(dil):
                        o_ref[r, :, dst] = dbuf[h, pl.ds(r, tm // dil, stride=dil), :].astype(BF16)


def _qkv(x, g, w, q_gain, k_gain, cos, sin):
    B, S, D = x.shape
    tm = min(QKV_TM, S)
    width = N_HEADS * HEAD_DIM
    return pl.pallas_call(
        functools.partial(_qkv_kernel, tm),
        grid=(B, S // tm),
        in_specs=[
            pl.BlockSpec((1, tm, D), lambda b, i: (b, i, 0)),
            _resident(g.shape), _resident(w.shape), _resident(q_gain.shape), _resident(k_gain.shape),
            pl.BlockSpec((tm, HEAD_DIM), lambda b, i: (i, 0)),
            pl.BlockSpec((tm, HEAD_DIM), lambda b, i: (i, 0)),
        ],
        out_specs=[pl.BlockSpec((None, dil, tm // dil, 3 * width), lambda b, i: (b, 0, i, 0))
                   for _, dil in GROUPS],
        out_shape=[jax.ShapeDtypeStruct((B, dil, S // dil, 3 * width), BF16) for _, dil in GROUPS],
        scratch_shapes=[pltpu.VMEM((tm, D), BF16), pltpu.VMEM((N_HEADS, tm, HEAD_DIM), F32)],
        compiler_params=pltpu.CompilerParams(
            dimension_semantics=("arbitrary", "arbitrary"), vmem_limit_bytes=VMEM_LIMIT),
        name="qkv",
    )(x, g, w, q_gain, k_gain, cos, sin)


def _attn_kernel(n_res, tl, sub_len, q_ref, kp_ref, kc_ref, kn_ref, vp_ref, vc_ref, vn_ref,
                 o_ref, st_ref, kbuf, vbuf):
    R = ATTN_REACH
    j = pl.program_id(2)
    qb = 2 * R
    nk = qb + 2 * R
    row = lax.broadcasted_iota(jnp.int32, (qb, nk), 0)
    col = lax.broadcasted_iota(jnp.int32, (qb, nk), 1)
    band = jnp.where(jnp.abs(col - row - R) <= R, 0.0, NEG_BIG)
    lane = lax.broadcasted_iota(jnp.int32, (qb, LANES), 1)
    ones = jnp.ones((nk, HEAD_DIM), BF16)
    n_sub = tl // qb
    for rr in range(n_res):
        kbuf[rr, 0:R, :] = kp_ref[rr]
        kbuf[rr, R:R + tl, :] = kc_ref[rr]
        kbuf[rr, R + tl:, :] = kn_ref[rr]
        vbuf[rr, 0:R, :] = vp_ref[rr]
        vbuf[rr, R:R + tl, :] = vc_ref[rr]
        vbuf[rr, R + tl:, :] = vn_ref[rr]
        for sb in range(n_sub):
            rows = slice(sb * qb, (sb + 1) * qb)
            keys = slice(sb * qb, sb * qb + nk)
            bias = band
            if sb == 0:
                bias = jnp.where(col + (j * tl - R) >= 0, bias, NEG_BIG)
            if sb == n_sub - 1:
                bias = jnp.where(col + (j * tl + sb * qb - R) < sub_len, bias, NEG_BIG)
            stats = jnp.zeros((qb, LANES), F32)
            for h in range(N_HEADS):
                sl = slice(h * HEAD_DIM, (h + 1) * HEAD_DIM)
                s = lax.dot_general(q_ref[rr, rows, sl], kbuf[rr, keys, sl], (((1,), (1,)), ((), ())),
                                    preferred_element_type=F32) + bias
                m = jnp.max(s, axis=-1, keepdims=True)
                p = jnp.exp(s - m).astype(BF16)
                v1 = jnp.concatenate([vbuf[rr, keys, sl], ones], axis=1)
                ol = jnp.dot(p, v1, preferred_element_type=F32)
                l = ol[:, HEAD_DIM:]
                o_ref[rr, rows, sl] = (ol[:, :HEAD_DIM] / l).astype(BF16)
                stats = jnp.where(lane == h, m + jnp.log(l), stats)
            st_ref[rr, rows, :] = stats


def _attn_group(qkv):
    B, dil, sub_len, _ = qkv.shape
    width = N_HEADS * HEAD_DIM
    tl = min(ATTN_TL, sub_len)
    n_res = min(dil, ATTN_TL // tl)
    R = ATTN_REACH
    per = tl // R
    n_rblocks = sub_len // R

    def cur(c, w=width):
        return pl.BlockSpec((None, n_res, tl, w), lambda b, r, j: (b, r, j, c))

    def prev(c):
        return pl.BlockSpec((None, n_res, R, width),
                            lambda b, r, j: (b, r, jnp.maximum(j * per - 1, 0), c))

    def nxt(c):
        return pl.BlockSpec((None, n_res, R, width),
                            lambda b, r, j: (b, r, jnp.minimum((j + 1) * per, n_rblocks - 1), c))

    return pl.pallas_call(
        functools.partial(_attn_kernel, n_res, tl, sub_len),
        grid=(B, dil // n_res, sub_len // tl),
        in_specs=[cur(0), prev(1), cur(1), nxt(1), prev(2), cur(2), nxt(2)],
        out_specs=[cur(0), cur(0, LANES)],
        out_shape=[jax.ShapeDtypeStruct((B, dil, sub_len, width), BF16),
                   jax.ShapeDtypeStruct((B, dil, sub_len, LANES), F32)],
        scratch_shapes=[pltpu.VMEM((n_res, tl + 2 * R, width), BF16),
                        pltpu.VMEM((n_res, tl + 2 * R, width), BF16)],
        compiler_params=pltpu.CompilerParams(
            dimension_semantics=("arbitrary", "arbitrary", "arbitrary"), vmem_limit_bytes=VMEM_LIMIT),
        name=f"attn_d{dil}",
    )(qkv, qkv, qkv, qkv, qkv, qkv, qkv)


def _merge_kernel(tm, x_ref, o0_ref, o1_ref, o2_ref, s0_ref, s1_ref, s2_ref, wo_ref, out_ref,
                  mbuf, obuf, sbuf):
    for gi, (o_ref, s_ref) in enumerate(((o1_ref, s1_ref), (o2_ref, s2_ref))):
        dil = GROUPS[gi + 1][1]
        for r in range(dil):
            rows = pl.ds(r, tm // dil, stride=dil)
            sbuf[gi, rows, :] = s_ref[r]
            for h in range(N_HEADS):
                obuf[gi, h, rows, :] = o_ref[r, :, h * HEAD_DIM:(h + 1) * HEAD_DIM].astype(F32)
    l0, l1, l2 = s0_ref[0], sbuf[0], sbuf[1]
    m = jnp.maximum(jnp.maximum(l0, l1), l2)
    e0, e1, e2 = jnp.exp(l0 - m), jnp.exp(l1 - m), jnp.exp(l2 - m)
    inv = 1.0 / (e0 + e1 + e2)
    w1, w2 = e1 * inv, e2 * inv
    for h in range(N_HEADS):
        sl = slice(h * HEAD_DIM, (h + 1) * HEAD_DIM)
        o0 = o0_ref[0, :, sl].astype(F32)
        oh = o0 + w1[:, h:h + 1] * (obuf[0, h] - o0) + w2[:, h:h + 1] * (obuf[1, h] - o0)
        mbuf[:, sl] = oh.astype(BF16)
    out_ref[0] = x_ref[0] + jnp.dot(mbuf[...], wo_ref[...], preferred_element_type=F32)


def _merge(x, outs, stats, w_o):
    B, S, D = x.shape
    tm = min(MERGE_TM, S)
    tok = lambda w: pl.BlockSpec((1, tm, w), lambda b, i: (b, i, 0))
    grp = lambda dil, w: pl.BlockSpec((None, dil, tm // dil, w), lambda b, i: (b, 0, i, 0))
    dils = [dil for _, dil in GROUPS]
    return pl.pallas_call(
        functools.partial(_merge_kernel, tm),
        grid=(B, S // tm),
        in_specs=([tok(D)] + [grp(dil, D) for dil in dils] + [grp(dil, LANES) for dil in dils]
                  + [_resident(w_o.shape)]),
        out_specs=tok(D),
        out_shape=jax.ShapeDtypeStruct((B, S, D), F32),
        scratch_shapes=[pltpu.VMEM((tm, D), BF16),
                        pltpu.VMEM((N_GROUPS - 1, N_HEADS, tm, HEAD_DIM), F32),
                        pltpu.VMEM((N_GROUPS - 1, tm, LANES), F32)],
        compiler_params=pltpu.CompilerParams(
            dimension_semantics=("arbitrary", "arbitrary"), vmem_limit_bytes=VMEM_LIMIT),
        name="attn_merge",
    )(x, *outs, *stats, w_o)


def _rope_tables(S):
    half = HEAD_DIM // 2
    inv = ROPE_THETA ** (-2.0 * jnp.arange(half, dtype=F32) / HEAD_DIM)
    ang = jnp.arange(S, dtype=F32)[:, None] * inv[None, :]
    cos, sin = jnp.cos(ang), jnp.sin(ang)
    return jnp.concatenate([cos, cos], axis=-1), jnp.concatenate([-sin, sin], axis=-1)


def _prep_gates(gate_w, gate_b, d):
    gw = (0.5 * jnp.concatenate([gate_w[d, 0], gate_w[d, 1]], axis=-1)).astype(BF16)
    return gw, 0.5 * gate_b[d].reshape(2, LRU_WIDTH)


def _trunk(x, p):
    row = lambda v: v.reshape(1, -1)

    def ffn_layer(i, k):
        return (row(p["norm_g"][i, k * 2]), *p["ffn"][i][k])

    xs = _ffn(x, [ffn_layer(0, 0)], interleave=True)
    lru_common = (row(p["norm_g"][0, 1]), p["lru_w_in"], row(p["lru_b_in"]), p["lru_conv_w"],
                  row(p["lru_conv_b"]))
    hb = _lru_pass(xs, True, *lru_common, *p["lru_gates"][1], row(p["lru_lambda"][1]))
    x = _lru_pass(xs, False, *lru_common, *p["lru_gates"][0], row(p["lru_lambda"][0]),
                  hb=hb, w_out=p["lru_w_out"], b_out=row(p["lru_b_out"]))
    x = _ffn(x, [ffn_layer(0, 1), ffn_layer(1, 0)])
    qkv = _qkv(x, row(p["norm_g"][1, 1]), p["attn_w_qkv"], p["attn_q_gain"], p["attn_k_gain"], *p["rope"])
    outs, stats = zip(*[_attn_group(t) for t in qkv])
    x = _merge(x, outs, stats, p["attn_w_o"])
    return _ffn(x, [ffn_layer(1, 1)])


def kernel(x_prompt, x_sample, norm_g, ffn_w_in, ffn_w_out, lru_w_in, lru_b_in, lru_conv_w, lru_conv_b, lru_gate_w, lru_gate_b, lru_lambda, lru_w_out, lru_b_out, attn_w_qkv, attn_q_gain, attn_k_gain, attn_w_o):
    p = {
        "norm_g": norm_g,
        "ffn": [[(ffn_w_in[i, k].astype(BF16), ffn_w_out[i, k].astype(BF16)) for k in range(2)]
                for i in range(2)],
        "lru_w_in": lru_w_in[0].astype(BF16),
        "lru_b_in": lru_b_in[0],
        "lru_conv_w": lru_conv_w[0],
        "lru_conv_b": lru_conv_b[0],
        "lru_gates": [_prep_gates(lru_gate_w[0], lru_gate_b[0], d) for d in range(2)],
        "lru_lambda": lru_lambda[0],
        "lru_w_out": lru_w_out[0].astype(BF16),
        "lru_b_out": lru_b_out[0],
        "attn_w_qkv": attn_w_qkv[0].astype(BF16),
        "attn_q_gain": attn_q_gain[0],
        "attn_k_gain": attn_k_gain[0],
        "attn_w_o": attn_w_o[0].astype(BF16),
        "rope": _rope_tables(max(x_prompt.shape[1], x_sample.shape[1])),
    }
    return _trunk(x_prompt, p), _trunk(x_sample, p)
```

```python
import functools

import jax
import jax.numpy as jnp
from jax import lax
from jax.experimental import pallas as pl
from jax.experimental.pallas import tpu as pltpu

F32 = jnp.float32
BF16 = jnp.bfloat16

D_MODEL = 1024
D_FF = 2816
LRU_WIDTH = 1024
LRU_BLOCKS = 8
LRU_BLOCK_W = LRU_WIDTH // LRU_BLOCKS
CONV_W = 4
CONV_LEFT = 2
LRU_C = 8.0
HEAD_DIM = 128
N_HEADS = D_MODEL // HEAD_DIM
GROUPS = ((128, 1), (512, 4), (2048, 16))
N_GROUPS = len(GROUPS)
ROPE_THETA = 10000.0
EPS = 1e-6

SUBLANES = 8
LANES = 128
MXU_DIM = 256
VMEM_LIMIT = 56 * 1024 * 1024

N_SLABS = D_MODEL // LANES
FFN_TM = 512
FFN_TF = MXU_DIM
FFN_NC = D_FF // FFN_TF
LRU_TL = FFN_TM
QKV_TM = 256
ATTN_REACH = 64
ATTN_TL = 512
MERGE_TM = 512
NEG_BIG = -1e30


def _resident(shape):
    nd = len(shape)
    return pl.BlockSpec(shape, lambda *_: (0,) * nd, pipeline_mode=pl.Buffered(1))


def _rmsnorm(x, g):
    return x * lax.rsqrt(jnp.mean(x * x, axis=-1, keepdims=True) + EPS) * g


def _from_slabs(ref):
    return jnp.concatenate([ref[n] for n in range(N_SLABS)], axis=1)


def _ffn_layer(x_ref, g_ref, win_ref, wo_ref, xn_ref, acc_ref, dst_ref, side_work=()):
    side_work = list(side_work)
    assert len(side_work) <= FFN_NC
    xn_ref[...] = _rmsnorm(x_ref[...], g_ref[...]).astype(BF16)
    for c in range(FFN_NC):
        if c < len(side_work):
            side_work[c]()
        cols = slice(c * FFN_TF, (c + 1) * FFN_TF)
        gate = jnp.dot(xn_ref[...], win_ref[:, cols], preferred_element_type=F32)
        up = jnp.dot(xn_ref[...], win_ref[:, D_FF + c * FFN_TF: D_FF + (c + 1) * FFN_TF],
                     preferred_element_type=F32)
        h = (gate * jax.nn.sigmoid(gate) * up).astype(BF16)
        part = jnp.dot(h, wo_ref[cols, :], preferred_element_type=F32)
        if c == 0:
            acc_ref[...] = part
        elif c < FFN_NC - 1:
            acc_ref[...] += part
        else:
            dst_ref[...] = x_ref[...] + 0.5 * (acc_ref[...] + part)


def _ffn_kernel(n_layers, interleave, x_ref, *refs):
    params = [refs[3 * k:3 * k + 3] for k in range(n_layers)]
    o_ref, xn_ref, acc_ref = refs[3 * n_layers:3 * n_layers + 3]
    mid_refs = refs[3 * n_layers + 3:]
    src = x_ref
    for k, (g_ref, win_ref, wo_ref) in enumerate(params):
        last = k == n_layers - 1
        dst = (acc_ref if interleave else o_ref) if last else mid_refs[k]
        _ffn_layer(src, g_ref, win_ref, wo_ref, xn_ref, acc_ref, dst)
        src = dst
    if interleave:
        chunk = acc_ref.shape[0] // SUBLANES
        for n in range(N_SLABS):
            for j in range(SUBLANES):
                o_ref[n, pl.ds(j, chunk, stride=SUBLANES), :] = (
                    acc_ref[j * chunk:(j + 1) * chunk, n * LANES:(n + 1) * LANES])


def _ffn(x, layers, interleave=False):
    B, S, D = x.shape
    T = B * S
    tm = min(FFN_TM, S)
    nts = S // tm
    if interleave:
        out_spec = pl.BlockSpec((None, N_SLABS, tm, LANES), lambda i: (i // nts, 0, i % nts, 0))
        out_shape = jax.ShapeDtypeStruct((B, N_SLABS, S, LANES), F32)
    else:
        out_spec = pl.BlockSpec((tm, D), lambda i: (i, 0))
        out_shape = jax.ShapeDtypeStruct((T, D), F32)
    flat = [a for layer in layers for a in layer]
    out = pl.pallas_call(
        functools.partial(_ffn_kernel, len(layers), interleave),
        grid=(T // tm,),
        in_specs=[pl.BlockSpec((tm, D), lambda i: (i, 0))] + [_resident(a.shape) for a in flat],
        out_specs=out_spec,
        out_shape=out_shape,
        scratch_shapes=([pltpu.VMEM((tm, D), BF16), pltpu.VMEM((tm, D), F32)]
                        + [pltpu.VMEM((tm, D), F32)] * (len(layers) - 1)),
        compiler_params=pltpu.CompilerParams(
            dimension_semantics=("arbitrary",), vmem_limit_bytes=VMEM_LIMIT),
        name="ffn" if len(layers) == 1 else "ffn_pair",
    )(x.reshape(T, D), *flat)
    return out if interleave else out.reshape(B, S, D)


def _lru_kernel(reverse, n_tiles, tl, *refs):
    if reverse:
        (xp_ref, xc_ref, xn_ref, g_ref, win_ref, bin_ref, cw_ref, cb_ref, gw_ref, gb_ref, lam_ref,
         o_ref, ubuf, abuf, bbuf, carry_ref) = refs
    else:
        (xp_ref, xc_ref, xn_ref, g_ref, win_ref, bin_ref, cw_ref, cb_ref, gw_ref, gb_ref, lam_ref,
         hb_ref, wout_ref, bout_ref,
         o_ref, ubuf, abuf, bbuf, carry_ref, ybuf, pbuf) = refs
    L = LRU_WIDTH
    S8 = SUBLANES
    n_off = tl // S8
    i = pl.program_id(1)
    t = (n_tiles - 1 - i) if reverse else i

    @pl.when(i == 0)
    def _():
        carry_ref[...] = jnp.zeros_like(carry_ref)

    g = g_ref[...]
    x = _from_slabs(xc_ref)
    hn = _rmsnorm(x, g).astype(BF16)
    b_in = bin_ref[...]
    if reverse:
        u_pre = jnp.dot(hn, win_ref[:, L:], preferred_element_type=F32) + b_in[:, L:]
    else:
        yu = jnp.dot(hn, win_ref[...], preferred_element_type=F32) + b_in
        ybuf[...] = jax.nn.gelu(yu[:, :L])
        u_pre = yu[:, L:]

    xh = jnp.concatenate([_from_slabs(xp_ref), _from_slabs(xn_ref)], axis=0)
    uh = jnp.dot(_rmsnorm(xh, g).astype(BF16), win_ref[:, L:], preferred_element_type=F32) + b_in[:, L:]
    u_m2 = jnp.where(t > 0, uh[S8 - 1:S8], 0.0)
    u_m1 = jnp.where(t > 0, uh[2 * S8 - 1:2 * S8], 0.0)
    u_p1 = jnp.where(t < n_tiles - 1, uh[2 * S8:2 * S8 + 1], 0.0)

    sub = lax.broadcasted_iota(jnp.int32, (S8, L), 0)
    ubuf[0:S8, :] = jnp.where(sub == 0, u_m2, pltpu.roll(u_pre[tl - 2 * S8:tl - S8], 1, 0))
    ubuf[S8:2 * S8, :] = jnp.where(sub == 0, u_m1, pltpu.roll(u_pre[tl - S8:tl], 1, 0))
    ubuf[2 * S8:2 * S8 + tl, :] = u_pre
    ubuf[2 * S8 + tl:, :] = jnp.where(sub == S8 - 1, u_p1, pltpu.roll(u_pre[0:S8], S8 - 1, 0))
    cw = cw_ref[...]
    u = cb_ref[...] + sum(ubuf[k * S8:k * S8 + tl, :] * cw[k:k + 1, :] for k in range(CONV_W))

    lam = lam_ref[...]
    half_log_a1 = (-0.5 * LRU_C) * jax.nn.softplus(-lam)
    gb = gb_ref[...]
    W = LRU_BLOCK_W
    for n in range(LRU_BLOCKS):
        sl = slice(n * W, (n + 1) * W)
        un = u[:, sl]
        gg = jnp.dot(un.astype(BF16), gw_ref[n], preferred_element_type=F32)
        th_r = jnp.tanh(gg[:, :W] + gb[0:1, sl])
        th_i = jnp.tanh(gg[:, W:] + gb[1:2, sl])
        k = half_log_a1[:, sl]
        a = jnp.exp(k + k * th_r)
        abuf[:, sl] = a
        bbuf[:, sl] = jnp.sqrt(1.0 - a * a) * ((0.5 * un) * (1.0 + th_i))

    def offset_step(k, carry):
        h, p = carry
        gi = (n_off - 1 - k) if reverse else k
        r0 = pl.multiple_of(gi * S8, S8)
        a = abuf[pl.ds(r0, S8), :]
        h = a * h + bbuf[pl.ds(r0, S8), :]
        p = a * p
        bbuf[pl.ds(r0, S8), :] = h
        abuf[pl.ds(r0, S8), :] = p
        return h, p

    hh, pp = lax.fori_loop(0, n_off, offset_step, (jnp.zeros((S8, L), F32), jnp.ones((S8, L), F32)),
                           unroll=4)
    for s in (1, 2, 4):
        if reverse:
            m, sh = sub < S8 - s, S8 - s
        else:
            m, sh = sub >= s, s
        h_sh = jnp.where(m, pltpu.roll(hh, sh, 0), 0.0)
        p_sh = jnp.where(m, pltpu.roll(pp, sh, 0), 1.0)
        hh = pp * h_sh + hh
        pp = pp * p_sh
    c_in = carry_ref[...]
    e = pp * c_in + hh
    if reverse:
        c = jnp.where(sub == S8 - 1, c_in, pltpu.roll(e, S8 - 1, 0))
        carry_ref[...] = jnp.broadcast_to(e[0:1, :], (S8, L))
    else:
        c = jnp.where(sub == 0, c_in, pltpu.roll(e, 1, 0))
        carry_ref[...] = jnp.broadcast_to(e[S8 - 1:S8, :], (S8, L))
    h = (bbuf[...].reshape(n_off, S8, L) + abuf[...].reshape(n_off, S8, L) * c[None]).reshape(tl, L)

    if reverse:
        o_ref[0] = h
    else:
        z = ((h + hb_ref[0]) * ybuf[...]).astype(BF16)
        res = x + jnp.dot(z, wout_ref[...], preferred_element_type=F32) + bout_ref[...]
        for n in range(N_SLABS):
            pbuf[n] = res[:, n * LANES:(n + 1) * LANES]
        for n in range(N_SLABS):
            for j in range(S8):
                o_ref[0, j * n_off:(j + 1) * n_off, n * LANES:(n + 1) * LANES] = (
                    pbuf[n, pl.ds(j, n_off, stride=S8), :])


def _lru_pass(x, reverse, g, w_in, b_in, conv_w, conv_b, gw, gb, lam, hb=None, w_out=None, b_out=None):
    B, _, S, _ = x.shape
    L = LRU_WIDTH
    D = D_MODEL
    tl = min(LRU_TL, S)
    nt = S // tl
    S8 = SUBLANES

    def tile(i):
        return (nt - 1 - i) if reverse else i

    x_specs = [
        pl.BlockSpec((None, N_SLABS, 2 * S8, LANES),
                     lambda b, i: (b, 0, jnp.maximum(tile(i) * (tl // (2 * S8)) - 1, 0), 0)),
        pl.BlockSpec((None, N_SLABS, tl, LANES), lambda b, i: (b, 0, tile(i), 0)),
        pl.BlockSpec((None, N_SLABS, S8, LANES),
                     lambda b, i: (b, 0, jnp.minimum((tile(i) + 1) * (tl // S8), S // S8 - 1), 0)),
    ]
    common = [g, w_in, b_in, conv_w, conv_b, gw, gb, lam]
    in_specs = x_specs + [_resident(a.shape) for a in common]
    args = [x, x, x] + common
    scratch = [pltpu.VMEM((tl + 3 * S8, L), F32), pltpu.VMEM((tl, L), F32),
               pltpu.VMEM((tl, L), F32), pltpu.VMEM((S8, L), F32)]
    if not reverse:
        in_specs += [pl.BlockSpec((1, tl, L), lambda b, i: (b, i, 0)),
                     _resident(w_out.shape), _resident(b_out.shape)]
        args += [hb, w_out, b_out]
        scratch += [pltpu.VMEM((tl, L), F32), pltpu.VMEM((N_SLABS, tl, LANES), F32)]
    return pl.pallas_call(
        functools.partial(_lru_kernel, reverse, nt, tl),
        grid=(B, nt),
        in_specs=in_specs,
        out_specs=pl.BlockSpec((1, tl, D), lambda b, i: (b, tile(i), 0)),
        out_shape=jax.ShapeDtypeStruct((B, S, D), F32),
        scratch_shapes=scratch,
        compiler_params=pltpu.CompilerParams(
            dimension_semantics=("arbitrary", "arbitrary"), vmem_limit_bytes=VMEM_LIMIT),
        name="lru_bwd" if reverse else "lru_fwd",
    )(*args)


def _qkv_kernel(tm, x_ref, g_ref, w_ref, qg_ref, kg_ref, cos_ref, sin_ref, o0_ref, o1_ref, o2_ref,
                hn_ref, dbuf):
    hn_ref[...] = _rmsnorm(x_ref[0], g_ref[...]).astype(BF16)
    cos = cos_ref[...]
    sin = sin_ref[...]
    width = N_HEADS * HEAD_DIM
    o_refs = (o0_ref, o1_ref, o2_ref)
    for gi, (_, dil) in enumerate(GROUPS):
        o_ref = o_refs[gi]
        for which in range(3):
            c0 = (gi * 3 + which) * width
            t = jnp.dot(hn_ref[...], w_ref[:, c0:c0 + width], preferred_element_type=F32)
            gain = (qg_ref, kg_ref, None)[which]
            post = HEAD_DIM ** -0.5 if which == 0 else 1.0
            for h in range(N_HEADS):
                th = t[:, h * HEAD_DIM:(h + 1) * HEAD_DIM]
                if gain is not None:
                    inv = lax.rsqrt(jnp.mean(th * th, axis=-1, keepdims=True) + EPS) * post
                    tn = th * inv * gain[gi:gi + 1, :]
                    th = tn * cos + pltpu.roll(tn, HEAD_DIM // 2, 1) * sin
                dst = slice(which * width + h * HEAD_DIM, which * width + (h + 1) * HEAD_DIM)
                if dil == 1:
                    o_ref[0, :, dst] = th.astype(BF16)
                else:
                    dbuf[h] = th
                    for r in range(dil):
                        o_ref[r, :, dst] = dbuf[h, pl.ds(r, tm // dil, stride=dil), :].astype(BF16)


def _qkv(x, g, w, q_gain, k_gain, cos, sin):
    B, S, D = x.shape
    tm = min(QKV_TM, S)
    width = N_HEADS * HEAD_DIM
    return pl.pallas_call(
        functools.partial(_qkv_kernel, tm),
        grid=(B, S // tm),
        in_specs=[
            pl.BlockSpec((1, tm, D), lambda b, i: (b, i, 0)),
            _resident(g.shape), _resident(w.shape), _resident(q_gain.shape), _resident(k_gain.shape),
            pl.BlockSpec((tm, HEAD_DIM), lambda b, i: (i, 0)),
            pl.BlockSpec((tm, HEAD_DIM), lambda b, i: (i, 0)),
        ],
        out_specs=[pl.BlockSpec((None, dil, tm // dil, 3 * width), lambda b, i: (b, 0, i, 0))
                   for _, dil in GROUPS],
        out_shape=[jax.ShapeDtypeStruct((B, dil, S // dil, 3 * width), BF16) for _, dil in GROUPS],
        scratch_shapes=[pltpu.VMEM((tm, D), BF16), pltpu.VMEM((N_HEADS, tm, HEAD_DIM), F32)],
        compiler_params=pltpu.CompilerParams(
            dimension_semantics=("arbitrary", "arbitrary"), vmem_limit_bytes=VMEM_LIMIT),
        name="qkv",
    )(x, g, w, q_gain, k_gain, cos, sin)


def _attn_kernel(n_res, tl, sub_len, q_ref, kp_ref, kc_ref, kn_ref, vp_ref, vc_ref, vn_ref,
                 o_ref, st_ref, kbuf, vbuf):
    R = ATTN_REACH
    j = pl.program_id(2)
    qb = 2 * R
    nk = qb + 2 * R
    row = lax.broadcasted_iota(jnp.int32, (qb, nk), 0)
    col = lax.broadcasted_iota(jnp.int32, (qb, nk), 1)
    band = jnp.where(jnp.abs(col - row - R) <= R, 0.0, NEG_BIG)
    lane = lax.broadcasted_iota(jnp.int32, (qb, LANES), 1)
    ones = jnp.ones((nk, HEAD_DIM), BF16)
    n_sub = tl // qb
    for rr in range(n_res):
        kbuf[rr, 0:R, :] = kp_ref[rr]
        kbuf[rr, R:R + tl, :] = kc_ref[rr]
        kbuf[rr, R + tl:, :] = kn_ref[rr]
        vbuf[rr, 0:R, :] = vp_ref[rr]
        vbuf[rr, R:R + tl, :] = vc_ref[rr]
        vbuf[rr, R + tl:, :] = vn_ref[rr]
        for sb in range(n_sub):
            rows = slice(sb * qb, (sb + 1) * qb)
            keys = slice(sb * qb, sb * qb + nk)
            bias = band
            if sb == 0:
                bias = jnp.where(col + (j * tl - R) >= 0, bias, NEG_BIG)
            if sb == n_sub - 1:
                bias = jnp.where(col + (j * tl + sb * qb - R) < sub_len, bias, NEG_BIG)
            stats = jnp.zeros((qb, LANES), F32)
            for h in range(N_HEADS):
                sl = slice(h * HEAD_DIM, (h + 1) * HEAD_DIM)
                s = lax.dot_general(q_ref[rr, rows, sl], kbuf[rr, keys, sl], (((1,), (1,)), ((), ())),
                                    preferred_element_type=F32) + bias
                m = jnp.max(s, axis=-1, keepdims=True)
                p = jnp.exp(s - m).astype(BF16)
                v1 = jnp.concatenate([vbuf[rr, keys, sl], ones], axis=1)
                ol = jnp.dot(p, v1, preferred_element_type=F32)
                l = ol[:, HEAD_DIM:]
                o_ref[rr, rows, sl] = (ol[:, :HEAD_DIM] / l).astype(BF16)
                stats = jnp.where(lane == h, m + jnp.log(l), stats)
            st_ref[rr, rows, :] = stats


def _attn_group(qkv):
    B, dil, sub_len, _ = qkv.shape
    width = N_HEADS * HEAD_DIM
    tl = min(ATTN_TL, sub_len)
    n_res = min(dil, ATTN_TL // tl)
    R = ATTN_REACH
    per = tl // R
    n_rblocks = sub_len // R

    def cur(c, w=width):
        return pl.BlockSpec((None, n_res, tl, w), lambda b, r, j: (b, r, j, c))

    def prev(c):
        return pl.BlockSpec((None, n_res, R, width),
                            lambda b, r, j: (b, r, jnp.maximum(j * per - 1, 0), c))

    def nxt(c):
        return pl.BlockSpec((None, n_res, R, width),
                            lambda b, r, j: (b, r, jnp.minimum((j + 1) * per, n_rblocks - 1), c))

    return pl.pallas_call(
        functools.partial(_attn_kernel, n_res, tl, sub_len),
        grid=(B, dil // n_res, sub_len // tl),
        in_specs=[cur(0), prev(1), cur(1), nxt(1), prev(2), cur(2), nxt(2)],
        out_specs=[cur(0), cur(0, LANES)],
        out_shape=[jax.ShapeDtypeStruct((B, dil, sub_len, width), BF16),
                   jax.ShapeDtypeStruct((B, dil, sub_len, LANES), F32)],
        scratch_shapes=[pltpu.VMEM((n_res, tl + 2 * R, width), BF16),
                        pltpu.VMEM((n_res, tl + 2 * R, width), BF16)],
        compiler_params=pltpu.CompilerParams(
            dimension_semantics=("arbitrary", "arbitrary", "arbitrary"), vmem_limit_bytes=VMEM_LIMIT),
        name=f"attn_d{dil}",
    )(qkv, qkv, qkv, qkv, qkv, qkv, qkv)


def _merge_steps(tm, x_ref, o0_ref, o1_ref, o2_ref, s0_ref, s1_ref, s2_ref, wo_ref, dst_ref, mbuf, obuf, sbuf,
                 wbuf):
    def regroup(gi, o_ref, s_ref):
        dil = GROUPS[gi + 1][1]
        for r in range(dil):
            rows = pl.ds(r, tm // dil, stride=dil)
            sbuf[gi, rows, :] = s_ref[r]
            for h in range(N_HEADS):
                obuf[gi, h, rows, :] = o_ref[r, :, h * HEAD_DIM:(h + 1) * HEAD_DIM].astype(F32)

    def weights():
        l0, l1, l2 = s0_ref[0], sbuf[0], sbuf[1]
        m = jnp.maximum(jnp.maximum(l0, l1), l2)
        e0, e1, e2 = jnp.exp(l0 - m), jnp.exp(l1 - m), jnp.exp(l2 - m)
        inv = 1.0 / (e0 + e1 + e2)
        wbuf[0] = e1 * inv
        wbuf[1] = e2 * inv

    def head(h):
        sl = slice(h * HEAD_DIM, (h + 1) * HEAD_DIM)
        o0 = o0_ref[0, :, sl].astype(F32)
        oh = o0 + wbuf[0, :, h:h + 1] * (obuf[0, h] - o0) + wbuf[1, :, h:h + 1] * (obuf[1, h] - o0)
        mbuf[:, sl] = oh.astype(BF16)

    def project():
        dst_ref[...] = x_ref[0] + jnp.dot(mbuf[...], wo_ref[...], preferred_element_type=F32)

    return ([functools.partial(regroup, 0, o1_ref, s1_ref), functools.partial(regroup, 1, o2_ref, s2_ref), weights]
            + [functools.partial(head, h) for h in range(N_HEADS)] + [project])


def _merge_ffn_kernel(tm, x_ref, o0_ref, o1_ref, o2_ref, s0_ref, s1_ref, s2_ref, wo_attn_ref,
                      g_ref, win_ref, wo_ref, out_ref, mbuf, obuf, sbuf, wbuf, xa_ref, xb_ref, xn_ref, acc_ref):
    i = pl.program_id(0)

    @pl.when(i == 0)
    def _():
        xb_ref[...] = jnp.zeros_like(xb_ref)

    def step(merged_ref, ready_ref):
        steps = _merge_steps(tm, x_ref, o0_ref, o1_ref, o2_ref, s0_ref, s1_ref, s2_ref, wo_attn_ref, merged_ref,
                             mbuf, obuf, sbuf, wbuf)
        first = lambda: (steps[0](), steps[1]())
        _ffn_layer(ready_ref, g_ref, win_ref, wo_ref, xn_ref, acc_ref, out_ref, side_work=[first] + steps[2:])

    @pl.when(i % 2 == 0)
    def _():
        step(xa_ref, xb_ref)

    @pl.when(i % 2 == 1)
    def _():
        step(xb_ref, xa_ref)


def _merge_ffn(x, outs, stats, w_o, ffn_layer):
    B, S, D = x.shape
    tm = min(MERGE_TM, S)
    nts = S // tm
    n_tiles = B * nts
    g, w_in, wo = ffn_layer

    def tile(i):
        s = jnp.minimum(i, n_tiles - 1)
        return s // nts, s % nts

    tok = lambda w: pl.BlockSpec((1, tm, w), lambda i: (*tile(i), 0))
    grp = lambda dil, w: pl.BlockSpec((None, dil, tm // dil, w), lambda i: (tile(i)[0], 0, tile(i)[1], 0))
    dils = [dil for _, dil in GROUPS]
    out = pl.pallas_call(
        functools.partial(_merge_ffn_kernel, tm),
        grid=(n_tiles + 1,),
        in_specs=([tok(D)] + [grp(dil, D) for dil in dils] + [grp(dil, LANES) for dil in dils]
                  + [_resident(a.shape) for a in (w_o, g, w_in, wo)]),
        out_specs=pl.BlockSpec((tm, D), lambda i: (jnp.maximum(i - 1, 0), 0)),
        out_shape=jax.ShapeDtypeStruct((B * S, D), F32),
        scratch_shapes=[pltpu.VMEM((tm, D), BF16),
                        pltpu.VMEM((N_GROUPS - 1, N_HEADS, tm, HEAD_DIM), F32),
                        pltpu.VMEM((N_GROUPS - 1, tm, LANES), F32),
                        pltpu.VMEM((N_GROUPS - 1, tm, LANES), F32),
                        pltpu.VMEM((tm, D), F32), pltpu.VMEM((tm, D), F32),
                        pltpu.VMEM((tm, D), BF16), pltpu.VMEM((tm, D), F32)],
        compiler_params=pltpu.CompilerParams(
            dimension_semantics=("arbitrary",), vmem_limit_bytes=VMEM_LIMIT),
        name="attn_merge_ffn",
    )(x, *outs, *stats, w_o, g, w_in, wo)
    return out.reshape(B, S, D)


def _rope_tables(S):
    half = HEAD_DIM // 2
    lo_n = LANES
    inv = ROPE_THETA ** (-2.0 * jnp.arange(half, dtype=F32) / HEAD_DIM)
    ang_hi = (jnp.arange(S // lo_n, dtype=F32) * lo_n)[:, None] * inv[None, :]
    ang_lo = jnp.arange(lo_n, dtype=F32)[:, None] * inv[None, :]
    ch, sh = jnp.cos(ang_hi)[:, None, :], jnp.sin(ang_hi)[:, None, :]
    cl, sl = jnp.cos(ang_lo)[None, :, :], jnp.sin(ang_lo)[None, :, :]
    cos = (ch * cl - sh * sl).reshape(S, half)
    sin = (sh * cl + ch * sl).reshape(S, half)
    return jnp.concatenate([cos, cos], axis=-1), jnp.concatenate([-sin, sin], axis=-1)


def _prep_gates(gate_w, gate_b, d):
    gw = (0.5 * jnp.concatenate([gate_w[d, 0], gate_w[d, 1]], axis=-1)).astype(BF16)
    return gw, 0.5 * gate_b[d].reshape(2, LRU_WIDTH)


def _trunk(x, p):
    row = lambda v: v.reshape(1, -1)

    def ffn_layer(i, k):
        return (row(p["norm_g"][i, k * 2]), *p["ffn"][i][k])

    xs = _ffn(x, [ffn_layer(0, 0)], interleave=True)
    lru_common = (row(p["norm_g"][0, 1]), p["lru_w_in"], row(p["lru_b_in"]), p["lru_conv_w"],
                  row(p["lru_conv_b"]))
    hb = _lru_pass(xs, True, *lru_common, *p["lru_gates"][1], row(p["lru_lambda"][1]))
    x = _lru_pass(xs, False, *lru_common, *p["lru_gates"][0], row(p["lru_lambda"][0]),
                  hb=hb, w_out=p["lru_w_out"], b_out=row(p["lru_b_out"]))
    x = _ffn(x, [ffn_layer(0, 1), ffn_layer(1, 0)])
    qkv = _qkv(x, row(p["norm_g"][1, 1]), p["attn_w_qkv"], p["attn_q_gain"], p["attn_k_gain"], *p["rope"])
    outs, stats = zip(*[_attn_group(t) for t in qkv])
    return _merge_ffn(x, outs, stats, p["attn_w_o"], ffn_layer(1, 1))


def kernel(x_prompt, x_sample, norm_g, ffn_w_in, ffn_w_out, lru_w_in, lru_b_in, lru_conv_w, lru_conv_b, lru_gate_w, lru_gate_b, lru_lambda, lru_w_out, lru_b_out, attn_w_qkv, attn_q_gain, attn_k_gain, attn_w_o):
    p = {
        "norm_g": norm_g,
        "ffn": [[(ffn_w_in[i, k].astype(BF16), ffn_w_out[i, k].astype(BF16)) for k in range(2)]
                for i in range(2)],
        "lru_w_in": lru_w_in[0].astype(BF16),
        "lru_b_in": lru_b_in[0],
        "lru_conv_w": lru_conv_w[0],
        "lru_conv_b": lru_conv_b[0],
        "lru_gates": [_prep_gates(lru_gate_w[0], lru_gate_b[0], d) for d in range(2)],
        "lru_lambda": lru_lambda[0],
        "lru_w_out": lru_w_out[0].astype(BF16),
        "lru_b_out": lru_b_out[0],
        "attn_w_qkv": attn_w_qkv[0].astype(BF16),
        "attn_q_gain": attn_q_gain[0],
        "attn_k_gain": attn_k_gain[0],
        "attn_w_o": attn_w_o[0].astype(BF16),
        "rope": _rope_tables(max(x_prompt.shape[1], x_sample.shape[1])),
    }
    return _trunk(x_prompt, p), _trunk(x_sample, p)
```

```python
import functools
import math

import jax
import jax.numpy as jnp
from jax import lax
from jax.experimental import pallas as pl
from jax.experimental.pallas import tpu as pltpu

F32 = jnp.float32
BF16 = jnp.bfloat16

D_MODEL = 1024
D_FF = 2816
LRU_WIDTH = 1024
LRU_BLOCKS = 8
LRU_BLOCK_W = LRU_WIDTH // LRU_BLOCKS
CONV_W = 4
CONV_LEFT = 2
LRU_C = 8.0
HEAD_DIM = 128
N_HEADS = D_MODEL // HEAD_DIM
GROUPS = ((128, 1), (512, 4), (2048, 16))
N_GROUPS = len(GROUPS)
ROPE_THETA = 10000.0
EPS = 1e-6

SUBLANES = 8
LANES = 128
MXU_DIM = 256
VMEM_LIMIT = 56 * 1024 * 1024

N_SLABS = D_MODEL // LANES
FFN_TM = 512
FFN_TF = MXU_DIM
FFN_NC = D_FF // FFN_TF
LRU_TL = FFN_TM
QKV_TM = 256
ATTN_REACH = 64
ATTN_TL = 512
MERGE_TM = 512
NEG_BIG = -1e30
LOG2_E = math.log2(math.e)


def _resident(shape):
    nd = len(shape)
    return pl.BlockSpec(shape, lambda *_: (0,) * nd, pipeline_mode=pl.Buffered(1))


def _rmsnorm(x, g):
    return x * lax.rsqrt(jnp.mean(x * x, axis=-1, keepdims=True) + EPS) * g


def _from_slabs(ref):
    return jnp.concatenate([ref[n] for n in range(N_SLABS)], axis=1)


def _ffn_layer(x_ref, g_ref, win_ref, wo_ref, xn_ref, acc_ref, dst_ref, side_work=()):
    side_work = list(side_work)
    assert len(side_work) <= FFN_NC
    xn_ref[...] = _rmsnorm(x_ref[...], g_ref[...]).astype(BF16)
    for c in range(FFN_NC):
        if c < len(side_work):
            side_work[c]()
        cols = slice(c * FFN_TF, (c + 1) * FFN_TF)
        gate = jnp.dot(xn_ref[...], win_ref[:, cols], preferred_element_type=F32)
        up = jnp.dot(xn_ref[...], win_ref[:, D_FF + c * FFN_TF: D_FF + (c + 1) * FFN_TF],
                     preferred_element_type=F32)
        h = (gate * jax.nn.sigmoid(gate) * up).astype(BF16)
        part = jnp.dot(h, wo_ref[cols, :], preferred_element_type=F32)
        if c == 0:
            acc_ref[...] = part
        elif c < FFN_NC - 1:
            acc_ref[...] += part
        else:
            dst_ref[...] = x_ref[...] + 0.5 * (acc_ref[...] + part)


def _ffn_kernel(n_layers, interleave, x_ref, *refs):
    params = [refs[3 * k:3 * k + 3] for k in range(n_layers)]
    o_ref, xn_ref, acc_ref = refs[3 * n_layers:3 * n_layers + 3]
    mid_refs = refs[3 * n_layers + 3:]
    src = x_ref
    for k, (g_ref, win_ref, wo_ref) in enumerate(params):
        last = k == n_layers - 1
        dst = (acc_ref if interleave else o_ref) if last else mid_refs[k]
        _ffn_layer(src, g_ref, win_ref, wo_ref, xn_ref, acc_ref, dst)
        src = dst
    if interleave:
        chunk = acc_ref.shape[0] // SUBLANES
        for n in range(N_SLABS):
            for j in range(SUBLANES):
                o_ref[n, pl.ds(j, chunk, stride=SUBLANES), :] = (
                    acc_ref[j * chunk:(j + 1) * chunk, n * LANES:(n + 1) * LANES])


def _ffn(x, layers, interleave=False):
    B, S, D = x.shape
    T = B * S
    tm = min(FFN_TM, S)
    nts = S // tm
    if interleave:
        out_spec = pl.BlockSpec((None, N_SLABS, tm, LANES), lambda i: (i // nts, 0, i % nts, 0))
        out_shape = jax.ShapeDtypeStruct((B, N_SLABS, S, LANES), F32)
    else:
        out_spec = pl.BlockSpec((tm, D), lambda i: (i, 0))
        out_shape = jax.ShapeDtypeStruct((T, D), F32)
    flat = [a for layer in layers for a in layer]
    out = pl.pallas_call(
        functools.partial(_ffn_kernel, len(layers), interleave),
        grid=(T // tm,),
        in_specs=[pl.BlockSpec((tm, D), lambda i: (i, 0))] + [_resident(a.shape) for a in flat],
        out_specs=out_spec,
        out_shape=out_shape,
        scratch_shapes=([pltpu.VMEM((tm, D), BF16), pltpu.VMEM((tm, D), F32)]
                        + [pltpu.VMEM((tm, D), F32)] * (len(layers) - 1)),
        compiler_params=pltpu.CompilerParams(
            dimension_semantics=("arbitrary",), vmem_limit_bytes=VMEM_LIMIT),
        name="ffn" if len(layers) == 1 else "ffn_pair",
    )(x.reshape(T, D), *flat)
    return out if interleave else out.reshape(B, S, D)


def _lru_kernel(reverse, n_tiles, tl, *refs):
    if reverse:
        (xp_ref, xc_ref, xn_ref, g_ref, win_ref, bin_ref, cw_ref, cb_ref, gw_ref, gb_ref, lam_ref,
         o_ref, ubuf, abuf, bbuf, carry_ref) = refs
    else:
        (xp_ref, xc_ref, xn_ref, g_ref, win_ref, bin_ref, cw_ref, cb_ref, gw_ref, gb_ref, lam_ref,
         hb_ref, wout_ref, bout_ref,
         o_ref, ubuf, abuf, bbuf, carry_ref, ybuf, pbuf) = refs
    L = LRU_WIDTH
    S8 = SUBLANES
    n_off = tl // S8
    i = pl.program_id(1)
    t = (n_tiles - 1 - i) if reverse else i

    @pl.when(i == 0)
    def _():
        carry_ref[...] = jnp.zeros_like(carry_ref)

    g = g_ref[...]
    x = _from_slabs(xc_ref)
    hn = _rmsnorm(x, g).astype(BF16)
    b_in = bin_ref[...]
    if reverse:
        u_pre = jnp.dot(hn, win_ref[:, L:], preferred_element_type=F32) + b_in[:, L:]
    else:
        yu = jnp.dot(hn, win_ref[...], preferred_element_type=F32) + b_in
        ybuf[...] = jax.nn.gelu(yu[:, :L])
        u_pre = yu[:, L:]

    xh = jnp.concatenate([_from_slabs(xp_ref), _from_slabs(xn_ref)], axis=0)
    uh = jnp.dot(_rmsnorm(xh, g).astype(BF16), win_ref[:, L:], preferred_element_type=F32) + b_in[:, L:]
    u_m2 = jnp.where(t > 0, uh[S8 - 1:S8], 0.0)
    u_m1 = jnp.where(t > 0, uh[2 * S8 - 1:2 * S8], 0.0)
    u_p1 = jnp.where(t < n_tiles - 1, uh[2 * S8:2 * S8 + 1], 0.0)

    sub = lax.broadcasted_iota(jnp.int32, (S8, L), 0)
    ubuf[0:S8, :] = jnp.where(sub == 0, u_m2, pltpu.roll(u_pre[tl - 2 * S8:tl - S8], 1, 0))
    ubuf[S8:2 * S8, :] = jnp.where(sub == 0, u_m1, pltpu.roll(u_pre[tl - S8:tl], 1, 0))
    ubuf[2 * S8:2 * S8 + tl, :] = u_pre
    ubuf[2 * S8 + tl:, :] = jnp.where(sub == S8 - 1, u_p1, pltpu.roll(u_pre[0:S8], S8 - 1, 0))
    cw = cw_ref[...]
    u = cb_ref[...] + sum(ubuf[k * S8:k * S8 + tl, :] * cw[k:k + 1, :] for k in range(CONV_W))

    lam = lam_ref[...]
    half_log_a1 = (-0.5 * LRU_C * LOG2_E) * jax.nn.softplus(-lam)
    gb = gb_ref[...]
    W = LRU_BLOCK_W
    for n in range(LRU_BLOCKS):
        sl = slice(n * W, (n + 1) * W)
        un = u[:, sl]
        gg = jnp.dot(un.astype(BF16), gw_ref[n], preferred_element_type=F32)
        th_r = jnp.tanh(gg[:, :W] + gb[0:1, sl])
        th_i = jnp.tanh(gg[:, W:] + gb[1:2, sl])
        k = half_log_a1[:, sl]
        a = jnp.exp2(k + k * th_r)
        abuf[:, sl] = a
        bbuf[:, sl] = jnp.sqrt(1.0 - a * a) * ((0.5 * un) * (1.0 + th_i))

    def offset_step(k, carry):
        h, p = carry
        gi = (n_off - 1 - k) if reverse else k
        r0 = pl.multiple_of(gi * S8, S8)
        a = abuf[pl.ds(r0, S8), :]
        h = a * h + bbuf[pl.ds(r0, S8), :]
        p = a * p
        bbuf[pl.ds(r0, S8), :] = h
        abuf[pl.ds(r0, S8), :] = p
        return h, p

    hh, pp = lax.fori_loop(0, n_off, offset_step, (jnp.zeros((S8, L), F32), jnp.ones((S8, L), F32)),
                           unroll=4)
    for s in (1, 2, 4):
        if reverse:
            m, sh = sub < S8 - s, S8 - s
        else:
            m, sh = sub >= s, s
        h_sh = jnp.where(m, pltpu.roll(hh, sh, 0), 0.0)
        p_sh = jnp.where(m, pltpu.roll(pp, sh, 0), 1.0)
        hh = pp * h_sh + hh
        pp = pp * p_sh
    c_in = carry_ref[...]
    e = pp * c_in + hh
    if reverse:
        c = jnp.where(sub == S8 - 1, c_in, pltpu.roll(e, S8 - 1, 0))
        carry_ref[...] = jnp.broadcast_to(e[0:1, :], (S8, L))
    else:
        c = jnp.where(sub == 0, c_in, pltpu.roll(e, 1, 0))
        carry_ref[...] = jnp.broadcast_to(e[S8 - 1:S8, :], (S8, L))
    h = (bbuf[...].reshape(n_off, S8, L) + abuf[...].reshape(n_off, S8, L) * c[None]).reshape(tl, L)

    if reverse:
        o_ref[0] = h
    else:
        z = ((h + hb_ref[0]) * ybuf[...]).astype(BF16)
        res = x + jnp.dot(z, wout_ref[...], preferred_element_type=F32) + bout_ref[...]
        for n in range(N_SLABS):
            pbuf[n] = res[:, n * LANES:(n + 1) * LANES]
        for n in range(N_SLABS):
            for j in range(S8):
                o_ref[0, j * n_off:(j + 1) * n_off, n * LANES:(n + 1) * LANES] = (
                    pbuf[n, pl.ds(j, n_off, stride=S8), :])


def _lru_pass(x, reverse, g, w_in, b_in, conv_w, conv_b, gw, gb, lam, hb=None, w_out=None, b_out=None):
    B, _, S, _ = x.shape
    L = LRU_WIDTH
    D = D_MODEL
    tl = min(LRU_TL, S)
    nt = S // tl
    S8 = SUBLANES

    def tile(i):
        return (nt - 1 - i) if reverse else i

    x_specs = [
        pl.BlockSpec((None, N_SLABS, 2 * S8, LANES),
                     lambda b, i: (b, 0, jnp.maximum(tile(i) * (tl // (2 * S8)) - 1, 0), 0)),
        pl.BlockSpec((None, N_SLABS, tl, LANES), lambda b, i: (b, 0, tile(i), 0)),
        pl.BlockSpec((None, N_SLABS, S8, LANES),
                     lambda b, i: (b, 0, jnp.minimum((tile(i) + 1) * (tl // S8), S // S8 - 1), 0)),
    ]
    common = [g, w_in, b_in, conv_w, conv_b, gw, gb, lam]
    in_specs = x_specs + [_resident(a.shape) for a in common]
    args = [x, x, x] + common
    scratch = [pltpu.VMEM((tl + 3 * S8, L), F32), pltpu.VMEM((tl, L), F32),
               pltpu.VMEM((tl, L), F32), pltpu.VMEM((S8, L), F32)]
    if not reverse:
        in_specs += [pl.BlockSpec((1, tl, L), lambda b, i: (b, i, 0)),
                     _resident(w_out.shape), _resident(b_out.shape)]
        args += [hb, w_out, b_out]
        scratch += [pltpu.VMEM((tl, L), F32), pltpu.VMEM((N_SLABS, tl, LANES), F32)]
    return pl.pallas_call(
        functools.partial(_lru_kernel, reverse, nt, tl),
        grid=(B, nt),
        in_specs=in_specs,
        out_specs=pl.BlockSpec((1, tl, D), lambda b, i: (b, tile(i), 0)),
        out_shape=jax.ShapeDtypeStruct((B, S, D), F32),
        scratch_shapes=scratch,
        compiler_params=pltpu.CompilerParams(
            dimension_semantics=("arbitrary", "arbitrary"), vmem_limit_bytes=VMEM_LIMIT),
        name="lru_bwd" if reverse else "lru_fwd",
    )(*args)


def _qkv_kernel(tm, x_ref, g_ref, w_ref, qg_ref, kg_ref, cos_ref, sin_ref, o0_ref, o1_ref, o2_ref,
                hn_ref, dbuf):
    hn_ref[...] = _rmsnorm(x_ref[0], g_ref[...]).astype(BF16)
    width = N_HEADS * HEAD_DIM
    o_refs = (o0_ref, o1_ref, o2_ref)
    for gi, (_, dil) in enumerate(GROUPS):
        o_ref = o_refs[gi]
        for which in range(3):
            c0 = (gi * 3 + which) * width
            t = jnp.dot(hn_ref[...], w_ref[:, c0:c0 + width], preferred_element_type=F32)
            gain = (qg_ref, kg_ref, None)[which]
            post = HEAD_DIM ** -0.5 * LOG2_E if which == 0 else 1.0
            for h in range(N_HEADS):
                th = t[:, h * HEAD_DIM:(h + 1) * HEAD_DIM]
                if gain is not None:
                    inv = lax.rsqrt(jnp.mean(th * th, axis=-1, keepdims=True) + EPS) * post
                    tn = th * inv * gain[gi:gi + 1, :]
                    th = tn * cos_ref[...] + pltpu.roll(tn, HEAD_DIM // 2, 1) * sin_ref[...]
                dst = slice(which * width + h * HEAD_DIM, which * width + (h + 1) * HEAD_DIM)
                if dil == 1:
                    o_ref[0, :, dst] = th.astype(BF16)
                else:
                    dbuf[h] = th
                    for r in range(dil):
                        o_ref[r, :, dst] = dbuf[h, pl.ds(r, tm // dil, stride=dil), :].astype(BF16)


def _qkv(x, g, w, q_gain, k_gain, cos, sin):
    B, S, D = x.shape
    tm = min(QKV_TM, S)
    width = N_HEADS * HEAD_DIM
    return pl.pallas_call(
        functools.partial(_qkv_kernel, tm),
        grid=(B, S // tm),
        in_specs=[
            pl.BlockSpec((1, tm, D), lambda b, i: (b, i, 0)),
            _resident(g.shape), _resident(w.shape), _resident(q_gain.shape), _resident(k_gain.shape),
            pl.BlockSpec((tm, HEAD_DIM), lambda b, i: (i, 0)),
            pl.BlockSpec((tm, HEAD_DIM), lambda b, i: (i, 0)),
        ],
        out_specs=[pl.BlockSpec((None, dil, tm // dil, 3 * width), lambda b, i: (b, 0, i, 0))
                   for _, dil in GROUPS],
        out_shape=[jax.ShapeDtypeStruct((B, dil, S // dil, 3 * width), BF16) for _, dil in GROUPS],
        scratch_shapes=[pltpu.VMEM((tm, D), BF16), pltpu.VMEM((N_HEADS, tm, HEAD_DIM), F32)],
        compiler_params=pltpu.CompilerParams(
            dimension_semantics=("arbitrary", "arbitrary"), vmem_limit_bytes=VMEM_LIMIT),
        name="qkv",
    )(x, g, w, q_gain, k_gain, cos, sin)


def _attn_kernel(n_res, tl, sub_len, q_ref, kp_ref, kc_ref, kn_ref, vp_ref, vc_ref, vn_ref,
                 o_ref, st_ref, kbuf, vbuf):
    R = ATTN_REACH
    j = pl.program_id(2)
    qb = 2 * R
    nk = qb + 2 * R
    row = lax.broadcasted_iota(jnp.int32, (qb, nk), 0)
    col = lax.broadcasted_iota(jnp.int32, (qb, nk), 1)
    band = jnp.where(jnp.abs(col - row - R) <= R, 0.0, NEG_BIG)
    lane = lax.broadcasted_iota(jnp.int32, (qb, LANES), 1)
    ones = jnp.ones((nk, HEAD_DIM), BF16)
    n_sub = tl // qb
    for rr in range(n_res):
        kbuf[rr, 0:R, :] = kp_ref[rr]
        kbuf[rr, R:R + tl, :] = kc_ref[rr]
        kbuf[rr, R + tl:, :] = kn_ref[rr]
        vbuf[rr, 0:R, :] = vp_ref[rr]
        vbuf[rr, R:R + tl, :] = vc_ref[rr]
        vbuf[rr, R + tl:, :] = vn_ref[rr]
        for sb in range(n_sub):
            rows = slice(sb * qb, (sb + 1) * qb)
            keys = slice(sb * qb, sb * qb + nk)
            bias = band
            if sb == 0:
                bias = jnp.where(col + (j * tl - R) >= 0, bias, NEG_BIG)
            if sb == n_sub - 1:
                bias = jnp.where(col + (j * tl + sb * qb - R) < sub_len, bias, NEG_BIG)
            stats = jnp.ones((qb, LANES), F32)
            for h in range(N_HEADS):
                sl = slice(h * HEAD_DIM, (h + 1) * HEAD_DIM)
                s = lax.dot_general(q_ref[rr, rows, sl], kbuf[rr, keys, sl], (((1,), (1,)), ((), ())),
                                    preferred_element_type=F32) + bias
                m = jnp.max(s, axis=-1, keepdims=True)
                p = jnp.exp2(s - m).astype(BF16)
                v1 = jnp.concatenate([vbuf[rr, keys, sl], ones], axis=1)
                ol = jnp.dot(p, v1, preferred_element_type=F32)
                o_ref[rr, rows, sl] = ol[:, :HEAD_DIM].astype(BF16)
                stats = jnp.where(lane == h, m, stats)
                stats = jnp.where(lane == N_HEADS + h, ol[:, HEAD_DIM:], stats)
            st_ref[rr, rows, :] = stats


def _attn_group(qkv):
    B, dil, sub_len, _ = qkv.shape
    width = N_HEADS * HEAD_DIM
    tl = min(ATTN_TL, sub_len)
    n_res = min(dil, ATTN_TL // tl)
    R = ATTN_REACH
    per = tl // R
    n_rblocks = sub_len // R

    def cur(c, w=width):
        return pl.BlockSpec((None, n_res, tl, w), lambda b, r, j: (b, r, j, c))

    def prev(c):
        return pl.BlockSpec((None, n_res, R, width),
                            lambda b, r, j: (b, r, jnp.maximum(j * per - 1, 0), c))

    def nxt(c):
        return pl.BlockSpec((None, n_res, R, width),
                            lambda b, r, j: (b, r, jnp.minimum((j + 1) * per, n_rblocks - 1), c))

    return pl.pallas_call(
        functools.partial(_attn_kernel, n_res, tl, sub_len),
        grid=(B, dil // n_res, sub_len // tl),
        in_specs=[cur(0), prev(1), cur(1), nxt(1), prev(2), cur(2), nxt(2)],
        out_specs=[cur(0), cur(0, LANES)],
        out_shape=[jax.ShapeDtypeStruct((B, dil, sub_len, width), BF16),
                   jax.ShapeDtypeStruct((B, dil, sub_len, LANES), F32)],
        scratch_shapes=[pltpu.VMEM((n_res, tl + 2 * R, width), BF16),
                        pltpu.VMEM((n_res, tl + 2 * R, width), BF16)],
        compiler_params=pltpu.CompilerParams(
            dimension_semantics=("arbitrary", "arbitrary", "arbitrary"), vmem_limit_bytes=VMEM_LIMIT),
        name=f"attn_d{dil}",
    )(qkv, qkv, qkv, qkv, qkv, qkv, qkv)


def _merge_steps(tm, x_ref, o0_ref, o1_ref, o2_ref, s0_ref, s1_ref, s2_ref, wo_ref, dst_ref, mbuf, obuf, sbuf,
                 wbuf):
    def regroup(gi, o_ref, s_ref):
        dil = GROUPS[gi + 1][1]
        for r in range(dil):
            rows = pl.ds(r, tm // dil, stride=dil)
            sbuf[gi, rows, :] = s_ref[r]
            for h in range(N_HEADS):
                obuf[gi, h, rows, :] = o_ref[r, :, h * HEAD_DIM:(h + 1) * HEAD_DIM].astype(F32)

    def weights():
        st = (s0_ref[0], sbuf[0], sbuf[1])
        m = jnp.maximum(jnp.maximum(st[0], st[1]), st[2])
        e = [jnp.exp2(s - m) for s in st]
        den = sum(eg * pltpu.roll(s, LANES - N_HEADS, 1) for eg, s in zip(e, st))
        lane = lax.broadcasted_iota(jnp.int32, den.shape, 1)
        inv = 1.0 / jnp.where(lane < N_HEADS, den, 1.0)
        for gi in range(N_GROUPS):
            wbuf[gi] = e[gi] * inv

    def head(h):
        sl = slice(h * HEAD_DIM, (h + 1) * HEAD_DIM)
        oh = (wbuf[0, :, h:h + 1] * o0_ref[0, :, sl].astype(F32) + wbuf[1, :, h:h + 1] * obuf[0, h]
              + wbuf[2, :, h:h + 1] * obuf[1, h])
        mbuf[:, sl] = oh.astype(BF16)

    def project():
        dst_ref[...] = x_ref[0] + jnp.dot(mbuf[...], wo_ref[...], preferred_element_type=F32)

    return ([functools.partial(regroup, 0, o1_ref, s1_ref), functools.partial(regroup, 1, o2_ref, s2_ref), weights]
            + [functools.partial(head, h) for h in range(N_HEADS)] + [project])


def _merge_ffn_kernel(tm, x_ref, o0_ref, o1_ref, o2_ref, s0_ref, s1_ref, s2_ref, wo_attn_ref,
                      g_ref, win_ref, wo_ref, out_ref, mbuf, obuf, sbuf, wbuf, xa_ref, xb_ref, xn_ref, acc_ref):
    i = pl.program_id(0)

    @pl.when(i == 0)
    def _():
        xb_ref[...] = jnp.zeros_like(xb_ref)

    def step(merged_ref, ready_ref):
        steps = _merge_steps(tm, x_ref, o0_ref, o1_ref, o2_ref, s0_ref, s1_ref, s2_ref, wo_attn_ref, merged_ref,
                             mbuf, obuf, sbuf, wbuf)
        first = lambda: (steps[0](), steps[1]())
        _ffn_layer(ready_ref, g_ref, win_ref, wo_ref, xn_ref, acc_ref, out_ref, side_work=[first] + steps[2:])

    @pl.when(i % 2 == 0)
    def _():
        step(xa_ref, xb_ref)

    @pl.when(i % 2 == 1)
    def _():
        step(xb_ref, xa_ref)


def _merge_ffn(x, outs, stats, w_o, ffn_layer):
    B, S, D = x.shape
    tm = min(MERGE_TM, S)
    nts = S // tm
    n_tiles = B * nts
    g, w_in, wo = ffn_layer

    def tile(i):
        s = jnp.minimum(i, n_tiles - 1)
        return s // nts, s % nts

    tok = lambda w: pl.BlockSpec((1, tm, w), lambda i: (*tile(i), 0))
    grp = lambda dil, w: pl.BlockSpec((None, dil, tm // dil, w), lambda i: (tile(i)[0], 0, tile(i)[1], 0))
    dils = [dil for _, dil in GROUPS]
    out = pl.pallas_call(
        functools.partial(_merge_ffn_kernel, tm),
        grid=(n_tiles + 1,),
        in_specs=([tok(D)] + [grp(dil, D) for dil in dils] + [grp(dil, LANES) for dil in dils]
                  + [_resident(a.shape) for a in (w_o, g, w_in, wo)]),
        out_specs=pl.BlockSpec((tm, D), lambda i: (jnp.maximum(i - 1, 0), 0)),
        out_shape=jax.ShapeDtypeStruct((B * S, D), F32),
        scratch_shapes=[pltpu.VMEM((tm, D), BF16),
                        pltpu.VMEM((N_GROUPS - 1, N_HEADS, tm, HEAD_DIM), F32),
                        pltpu.VMEM((N_GROUPS - 1, tm, LANES), F32),
                        pltpu.VMEM((N_GROUPS, tm, LANES), F32),
                        pltpu.VMEM((tm, D), F32), pltpu.VMEM((tm, D), F32),
                        pltpu.VMEM((tm, D), BF16), pltpu.VMEM((tm, D), F32)],
        compiler_params=pltpu.CompilerParams(
            dimension_semantics=("arbitrary",), vmem_limit_bytes=VMEM_LIMIT),
        name="attn_merge_ffn",
    )(x, *outs, *stats, w_o, g, w_in, wo)
    return out.reshape(B, S, D)


def _rope_tables(S):
    half = HEAD_DIM // 2
    lo_n = LANES
    inv = ROPE_THETA ** (-2.0 * jnp.arange(half, dtype=F32) / HEAD_DIM)
    ang_hi = (jnp.arange(S // lo_n, dtype=F32) * lo_n)[:, None] * inv[None, :]
    ang_lo = jnp.arange(lo_n, dtype=F32)[:, None] * inv[None, :]
    ch, sh = jnp.cos(ang_hi)[:, None, :], jnp.sin(ang_hi)[:, None, :]
    cl, sl = jnp.cos(ang_lo)[None, :, :], jnp.sin(ang_lo)[None, :, :]
    cos = (ch * cl - sh * sl).reshape(S, half)
    sin = (sh * cl + ch * sl).reshape(S, half)
    return jnp.concatenate([cos, cos], axis=-1), jnp.concatenate([-sin, sin], axis=-1)


def _prep_gates(gate_w, gate_b, d):
    gw = (0.5 * jnp.concatenate([gate_w[d, 0], gate_w[d, 1]], axis=-1)).astype(BF16)
    return gw, 0.5 * gate_b[d].reshape(2, LRU_WIDTH)


def _trunk(x, p):
    row = lambda v: v.reshape(1, -1)

    def ffn_layer(i, k):
        return (row(p["norm_g"][i, k * 2]), *p["ffn"][i][k])

    xs = _ffn(x, [ffn_layer(0, 0)], interleave=True)
    lru_common = (row(p["norm_g"][0, 1]), p["lru_w_in"], row(p["lru_b_in"]), p["lru_conv_w"],
                  row(p["lru_conv_b"]))
    hb = _lru_pass(xs, True, *lru_common, *p["lru_gates"][1], row(p["lru_lambda"][1]))
    x = _lru_pass(xs, False, *lru_common, *p["lru_gates"][0], row(p["lru_lambda"][0]),
                  hb=hb, w_out=p["lru_w_out"], b_out=row(p["lru_b_out"]))
    x = _ffn(x, [ffn_layer(0, 1), ffn_layer(1, 0)])
    qkv = _qkv(x, row(p["norm_g"][1, 1]), p["attn_w_qkv"], p["attn_q_gain"], p["attn_k_gain"], *p["rope"])
    outs, stats = zip(*[_attn_group(t) for t in qkv])
    return _merge_ffn(x, outs, stats, p["attn_w_o"], ffn_layer(1, 1))


def kernel(x_prompt, x_sample, norm_g, ffn_w_in, ffn_w_out, lru_w_in, lru_b_in, lru_conv_w, lru_conv_b, lru_gate_w, lru_gate_b, lru_lambda, lru_w_out, lru_b_out, attn_w_qkv, attn_q_gain, attn_k_gain, attn_w_o):
    p = {
        "norm_g": norm_g,
        "ffn": [[(ffn_w_in[i, k].astype(BF16), ffn_w_out[i, k].astype(BF16)) for k in range(2)]
                for i in range(2)],
        "lru_w_in": lru_w_in[0].astype(BF16),
        "lru_b_in": lru_b_in[0],
        "lru_conv_w": lru_conv_w[0],
        "lru_conv_b": lru_conv_b[0],
        "lru_gates": [_prep_gates(lru_gate_w[0], lru_gate_b[0], d) for d in range(2)],
        "lru_lambda": lru_lambda[0],
        "lru_w_out": lru_w_out[0].astype(BF16),
        "lru_b_out": lru_b_out[0],
        "attn_w_qkv": attn_w_qkv[0].astype(BF16),
        "attn_q_gain": attn_q_gain[0],
        "attn_k_gain": attn_k_gain[0],
        "attn_w_o": attn_w_o[0].astype(BF16),
        "rope": _rope_tables(max(x_prompt.shape[1], x_sample.shape[1])),
    }
    return _trunk(x_prompt, p), _trunk(x_sample, p)
```

```python
import functools
import math

import jax
import jax.numpy as jnp
from jax import lax
from jax.experimental import pallas as pl
from jax.experimental.pallas import tpu as pltpu

F32 = jnp.float32
BF16 = jnp.bfloat16

D_MODEL = 1024
D_FF = 2816
LRU_WIDTH = 1024
LRU_BLOCKS = 8
LRU_BLOCK_W = LRU_WIDTH // LRU_BLOCKS
CONV_W = 4
CONV_LEFT = 2
LRU_C = 8.0
HEAD_DIM = 128
N_HEADS = D_MODEL // HEAD_DIM
GROUPS = ((128, 1), (512, 4), (2048, 16))
N_GROUPS = len(GROUPS)
ROPE_THETA = 10000.0
EPS = 1e-6

SUBLANES = 8
LANES = 128
MXU_DIM = 256
VMEM_LIMIT = 56 * 1024 * 1024

N_SLABS = D_MODEL // LANES
FFN_TM = 512
FFN_TF = MXU_DIM
FFN_NC = D_FF // FFN_TF
LRU_TL = FFN_TM
QKV_TM = 256
ATTN_REACH = 64
ATTN_TL = 512
MERGE_TM = 512
NEG_BIG = -1e30
LOG2_E = math.log2(math.e)


def _resident(shape):
    nd = len(shape)
    return pl.BlockSpec(shape, lambda *_: (0,) * nd, pipeline_mode=pl.Buffered(1))


def _rmsnorm(x, g):
    return x * lax.rsqrt(jnp.mean(x * x, axis=-1, keepdims=True) + EPS) * g


def _from_slabs(ref):
    return jnp.concatenate([ref[n] for n in range(N_SLABS)], axis=1)


def _ffn_layer(x_ref, g_ref, win_ref, wo_ref, xn_ref, acc_ref, dst_ref, side_work=()):
    side_work = list(side_work)
    assert len(side_work) <= FFN_NC
    xn_ref[...] = _rmsnorm(x_ref[...], g_ref[...]).astype(BF16)
    for c in range(FFN_NC):
        if c < len(side_work):
            side_work[c]()
        cols = slice(c * FFN_TF, (c + 1) * FFN_TF)
        gate = jnp.dot(xn_ref[...], win_ref[:, cols], preferred_element_type=F32)
        up = jnp.dot(xn_ref[...], win_ref[:, D_FF + c * FFN_TF: D_FF + (c + 1) * FFN_TF],
                     preferred_element_type=F32)
        h = (gate * jax.nn.sigmoid(gate) * up).astype(BF16)
        part = jnp.dot(h, wo_ref[cols, :], preferred_element_type=F32)
        if c == 0:
            acc_ref[...] = part
        elif c < FFN_NC - 1:
            acc_ref[...] += part
        else:
            dst_ref[...] = x_ref[...] + 0.5 * (acc_ref[...] + part)


def _ffn_kernel(n_layers, interleave, x_ref, *refs):
    params = [refs[3 * k:3 * k + 3] for k in range(n_layers)]
    o_ref, xn_ref, acc_ref = refs[3 * n_layers:3 * n_layers + 3]
    mid_refs = refs[3 * n_layers + 3:]
    src = x_ref
    for k, (g_ref, win_ref, wo_ref) in enumerate(params):
        last = k == n_layers - 1
        dst = (acc_ref if interleave else o_ref) if last else mid_refs[k]
        _ffn_layer(src, g_ref, win_ref, wo_ref, xn_ref, acc_ref, dst)
        src = dst
    if interleave:
        chunk = acc_ref.shape[0] // SUBLANES
        for n in range(N_SLABS):
            for j in range(SUBLANES):
                o_ref[n, pl.ds(j, chunk, stride=SUBLANES), :] = (
                    acc_ref[j * chunk:(j + 1) * chunk, n * LANES:(n + 1) * LANES])


def _ffn(x, layers, interleave=False):
    B, S, D = x.shape
    T = B * S
    tm = min(FFN_TM, S)
    nts = S // tm
    if interleave:
        out_spec = pl.BlockSpec((None, N_SLABS, tm, LANES), lambda i: (i // nts, 0, i % nts, 0))
        out_shape = jax.ShapeDtypeStruct((B, N_SLABS, S, LANES), F32)
    else:
        out_spec = pl.BlockSpec((tm, D), lambda i: (i, 0))
        out_shape = jax.ShapeDtypeStruct((T, D), F32)
    flat = [a for layer in layers for a in layer]
    out = pl.pallas_call(
        functools.partial(_ffn_kernel, len(layers), interleave),
        grid=(T // tm,),
        in_specs=[pl.BlockSpec((tm, D), lambda i: (i, 0))] + [_resident(a.shape) for a in flat],
        out_specs=out_spec,
        out_shape=out_shape,
        scratch_shapes=([pltpu.VMEM((tm, D), BF16), pltpu.VMEM((tm, D), F32)]
                        + [pltpu.VMEM((tm, D), F32)] * (len(layers) - 1)),
        compiler_params=pltpu.CompilerParams(
            dimension_semantics=("arbitrary",), vmem_limit_bytes=VMEM_LIMIT),
        name="ffn" if len(layers) == 1 else "ffn_pair",
    )(x.reshape(T, D), *flat)
    return out if interleave else out.reshape(B, S, D)


def _lru_kernel(reverse, n_tiles, tl, *refs):
    if reverse:
        (xp_ref, xc_ref, xn_ref, g_ref, win_ref, bin_ref, cw_ref, cb_ref, gw_ref, gb_ref, lam_ref,
         o_ref, u_ref, ubuf, abuf, bbuf, carry_ref) = refs
    else:
        (xc_ref, u_ref, g_ref, win_ref, bin_ref, gw_ref, gb_ref, lam_ref, hb_ref, wout_ref, bout_ref,
         o_ref, abuf, bbuf, carry_ref, ybuf, pbuf) = refs
    L = LRU_WIDTH
    S8 = SUBLANES
    n_off = tl // S8
    i = pl.program_id(1)
    t = (n_tiles - 1 - i) if reverse else i

    @pl.when(i == 0)
    def _():
        carry_ref[...] = jnp.zeros_like(carry_ref)

    g = g_ref[...]
    x = _from_slabs(xc_ref)
    hn = _rmsnorm(x, g).astype(BF16)
    b_in = bin_ref[...]
    sub = lax.broadcasted_iota(jnp.int32, (S8, L), 0)
    if reverse:
        u_pre = jnp.dot(hn, win_ref[:, L:], preferred_element_type=F32) + b_in[:, L:]

        xh = jnp.concatenate([_from_slabs(xp_ref), _from_slabs(xn_ref)], axis=0)
        uh = jnp.dot(_rmsnorm(xh, g).astype(BF16), win_ref[:, L:], preferred_element_type=F32) + b_in[:, L:]
        u_m2 = jnp.where(t > 0, uh[S8 - 1:S8], 0.0)
        u_m1 = jnp.where(t > 0, uh[2 * S8 - 1:2 * S8], 0.0)
        u_p1 = jnp.where(t < n_tiles - 1, uh[2 * S8:2 * S8 + 1], 0.0)

        ubuf[0:S8, :] = jnp.where(sub == 0, u_m2, pltpu.roll(u_pre[tl - 2 * S8:tl - S8], 1, 0))
        ubuf[S8:2 * S8, :] = jnp.where(sub == 0, u_m1, pltpu.roll(u_pre[tl - S8:tl], 1, 0))
        ubuf[2 * S8:2 * S8 + tl, :] = u_pre
        ubuf[2 * S8 + tl:, :] = jnp.where(sub == S8 - 1, u_p1, pltpu.roll(u_pre[0:S8], S8 - 1, 0))
        cw = cw_ref[...]
        u = cb_ref[...] + sum(ubuf[k * S8:k * S8 + tl, :] * cw[k:k + 1, :] for k in range(CONV_W))
        u_ref[0] = u
    else:
        ybuf[...] = jax.nn.gelu(jnp.dot(hn, win_ref[:, :L], preferred_element_type=F32) + b_in[:, :L])
        u = u_ref[0]

    lam = lam_ref[...]
    half_log_a1 = (-0.5 * LRU_C * LOG2_E) * jax.nn.softplus(-lam)
    gb = gb_ref[...]
    W = LRU_BLOCK_W
    for n in range(LRU_BLOCKS):
        sl = slice(n * W, (n + 1) * W)
        un = u[:, sl]
        gg = jnp.dot(un.astype(BF16), gw_ref[n], preferred_element_type=F32)
        th_r = jnp.tanh(gg[:, :W] + gb[0:1, sl])
        th_i = jnp.tanh(gg[:, W:] + gb[1:2, sl])
        k = half_log_a1[:, sl]
        a = jnp.exp2(k + k * th_r)
        abuf[:, sl] = a
        bbuf[:, sl] = jnp.sqrt(1.0 - a * a) * ((0.5 * un) * (1.0 + th_i))

    def offset_step(k, carry):
        h, p = carry
        gi = (n_off - 1 - k) if reverse else k
        r0 = pl.multiple_of(gi * S8, S8)
        a = abuf[pl.ds(r0, S8), :]
        h = a * h + bbuf[pl.ds(r0, S8), :]
        p = a * p
        bbuf[pl.ds(r0, S8), :] = h
        abuf[pl.ds(r0, S8), :] = p
        return h, p

    hh, pp = lax.fori_loop(0, n_off, offset_step, (jnp.zeros((S8, L), F32), jnp.ones((S8, L), F32)),
                           unroll=4)
    for s in (1, 2, 4):
        if reverse:
            m, sh = sub < S8 - s, S8 - s
        else:
            m, sh = sub >= s, s
        h_sh = jnp.where(m, pltpu.roll(hh, sh, 0), 0.0)
        p_sh = jnp.where(m, pltpu.roll(pp, sh, 0), 1.0)
        hh = pp * h_sh + hh
        pp = pp * p_sh
    c_in = carry_ref[...]
    e = pp * c_in + hh
    if reverse:
        c = jnp.where(sub == S8 - 1, c_in, pltpu.roll(e, S8 - 1, 0))
        carry_ref[...] = jnp.broadcast_to(e[0:1, :], (S8, L))
    else:
        c = jnp.where(sub == 0, c_in, pltpu.roll(e, 1, 0))
        carry_ref[...] = jnp.broadcast_to(e[S8 - 1:S8, :], (S8, L))
    h = (bbuf[...].reshape(n_off, S8, L) + abuf[...].reshape(n_off, S8, L) * c[None]).reshape(tl, L)

    if reverse:
        o_ref[0] = h
    else:
        z = ((h + hb_ref[0]) * ybuf[...]).astype(BF16)
        res = x + jnp.dot(z, wout_ref[...], preferred_element_type=F32) + bout_ref[...]
        for n in range(N_SLABS):
            pbuf[n] = res[:, n * LANES:(n + 1) * LANES]
        for n in range(N_SLABS):
            for j in range(S8):
                o_ref[0, j * n_off:(j + 1) * n_off, n * LANES:(n + 1) * LANES] = (
                    pbuf[n, pl.ds(j, n_off, stride=S8), :])


def _lru_pass(x, reverse, g, w_in, b_in, gw, gb, lam, conv_w=None, conv_b=None, u=None, hb=None, w_out=None,
              b_out=None):
    B, _, S, _ = x.shape
    L = LRU_WIDTH
    D = D_MODEL
    tl = min(LRU_TL, S)
    nt = S // tl
    S8 = SUBLANES

    def tile(i):
        return (nt - 1 - i) if reverse else i

    x_cur = pl.BlockSpec((None, N_SLABS, tl, LANES), lambda b, i: (b, 0, tile(i), 0))
    rows = pl.BlockSpec((1, tl, L), lambda b, i: (b, tile(i), 0))
    if reverse:
        x_prev = pl.BlockSpec((None, N_SLABS, 2 * S8, LANES),
                              lambda b, i: (b, 0, jnp.maximum(tile(i) * (tl // (2 * S8)) - 1, 0), 0))
        x_next = pl.BlockSpec((None, N_SLABS, S8, LANES),
                              lambda b, i: (b, 0, jnp.minimum((tile(i) + 1) * (tl // S8), S // S8 - 1), 0))
        resident = [g, w_in, b_in, conv_w, conv_b, gw, gb, lam]
        in_specs = [x_prev, x_cur, x_next] + [_resident(a.shape) for a in resident]
        args = [x, x, x] + resident
        out_specs = [rows, rows]
        out_shape = [jax.ShapeDtypeStruct((B, S, L), F32)] * 2
        scratch = [pltpu.VMEM((tl + 3 * S8, L), F32), pltpu.VMEM((tl, L), F32),
                   pltpu.VMEM((tl, L), F32), pltpu.VMEM((S8, L), F32)]
    else:
        resident = [g, w_in, b_in, gw, gb, lam]
        in_specs = ([x_cur, rows] + [_resident(a.shape) for a in resident]
                    + [rows, _resident(w_out.shape), _resident(b_out.shape)])
        args = [x, u] + resident + [hb, w_out, b_out]
        out_specs = rows
        out_shape = jax.ShapeDtypeStruct((B, S, D), F32)
        scratch = [pltpu.VMEM((tl, L), F32), pltpu.VMEM((tl, L), F32), pltpu.VMEM((S8, L), F32),
                   pltpu.VMEM((tl, L), F32), pltpu.VMEM((N_SLABS, tl, LANES), F32)]
    return pl.pallas_call(
        functools.partial(_lru_kernel, reverse, nt, tl),
        grid=(B, nt),
        in_specs=in_specs,
        out_specs=out_specs,
        out_shape=out_shape,
        scratch_shapes=scratch,
        compiler_params=pltpu.CompilerParams(
            dimension_semantics=("arbitrary", "arbitrary"), vmem_limit_bytes=VMEM_LIMIT),
        name="lru_bwd" if reverse else "lru_fwd",
    )(*args)


def _qkv_kernel(tm, x_ref, g_ref, w_ref, qg_ref, kg_ref, cos_ref, sin_ref, o0_ref, o1_ref, o2_ref,
                hn_ref, dbuf):
    hn_ref[...] = _rmsnorm(x_ref[0], g_ref[...]).astype(BF16)
    width = N_HEADS * HEAD_DIM
    o_refs = (o0_ref, o1_ref, o2_ref)
    for gi, (_, dil) in enumerate(GROUPS):
        o_ref = o_refs[gi]
        for which in range(3):
            c0 = (gi * 3 + which) * width
            t = jnp.dot(hn_ref[...], w_ref[:, c0:c0 + width], preferred_element_type=F32)
            gain = (qg_ref, kg_ref, None)[which]
            post = HEAD_DIM ** -0.5 * LOG2_E if which == 0 else 1.0
            for h in range(N_HEADS):
                th = t[:, h * HEAD_DIM:(h + 1) * HEAD_DIM]
                if gain is not None:
                    inv = lax.rsqrt(jnp.mean(th * th, axis=-1, keepdims=True) + EPS) * post
                    tn = th * inv * gain[gi:gi + 1, :]
                    th = tn * cos_ref[...] + pltpu.roll(tn, HEAD_DIM // 2, 1) * sin_ref[...]
                dst = slice(which * width + h * HEAD_DIM, which * width + (h + 1) * HEAD_DIM)
                if dil == 1:
                    o_ref[0, :, dst] = th.astype(BF16)
                else:
                    dbuf[h] = th
                    for r in range(dil):
                        o_ref[r, :, dst] = dbuf[h, pl.ds(r, tm // dil, stride=dil), :].astype(BF16)


def _qkv(x, g, w, q_gain, k_gain, cos, sin):
    B, S, D = x.shape
    tm = min(QKV_TM, S)
    width = N_HEADS * HEAD_DIM
    return pl.pallas_call(
        functools.partial(_qkv_kernel, tm),
        grid=(B, S // tm),
        in_specs=[
            pl.BlockSpec((1, tm, D), lambda b, i: (b, i, 0)),
            _resident(g.shape), _resident(w.shape), _resident(q_gain.shape), _resident(k_gain.shape),
            pl.BlockSpec((tm, HEAD_DIM), lambda b, i: (i, 0)),
            pl.BlockSpec((tm, HEAD_DIM), lambda b, i: (i, 0)),
        ],
        out_specs=[pl.BlockSpec((None, dil, tm // dil, 3 * width), lambda b, i: (b, 0, i, 0))
                   for _, dil in GROUPS],
        out_shape=[jax.ShapeDtypeStruct((B, dil, S // dil, 3 * width), BF16) for _, dil in GROUPS],
        scratch_shapes=[pltpu.VMEM((tm, D), BF16), pltpu.VMEM((N_HEADS, tm, HEAD_DIM), F32)],
        compiler_params=pltpu.CompilerParams(
            dimension_semantics=("arbitrary", "arbitrary"), vmem_limit_bytes=VMEM_LIMIT),
        name="qkv",
    )(x, g, w, q_gain, k_gain, cos, sin)


def _attn_kernel(n_res, tl, sub_len, q_ref, kp_ref, kc_ref, kn_ref, vp_ref, vc_ref, vn_ref,
                 o_ref, st_ref, kbuf, vbuf):
    R = ATTN_REACH
    j = pl.program_id(2)
    qb = 2 * R
    nk = qb + 2 * R
    row = lax.broadcasted_iota(jnp.int32, (qb, nk), 0)
    col = lax.broadcasted_iota(jnp.int32, (qb, nk), 1)
    band = jnp.where(jnp.abs(col - row - R) <= R, 0.0, NEG_BIG)
    lane = lax.broadcasted_iota(jnp.int32, (qb, LANES), 1)
    ones = jnp.ones((nk, HEAD_DIM), BF16)
    n_sub = tl // qb
    for rr in range(n_res):
        kbuf[rr, 0:R, :] = kp_ref[rr]
        kbuf[rr, R:R + tl, :] = kc_ref[rr]
        kbuf[rr, R + tl:, :] = kn_ref[rr]
        vbuf[rr, 0:R, :] = vp_ref[rr]
        vbuf[rr, R:R + tl, :] = vc_ref[rr]
        vbuf[rr, R + tl:, :] = vn_ref[rr]
        for sb in range(n_sub):
            rows = slice(sb * qb, (sb + 1) * qb)
            keys = slice(sb * qb, sb * qb + nk)
            bias = band
            if sb == 0:
                bias = jnp.where(col + (j * tl - R) >= 0, bias, NEG_BIG)
            if sb == n_sub - 1:
                bias = jnp.where(col + (j * tl + sb * qb - R) < sub_len, bias, NEG_BIG)
            stats = jnp.ones((qb, LANES), F32)
            for h in range(N_HEADS):
                sl = slice(h * HEAD_DIM, (h + 1) * HEAD_DIM)
                s = lax.dot_general(q_ref[rr, rows, sl], kbuf[rr, keys, sl], (((1,), (1,)), ((), ())),
                                    preferred_element_type=F32) + bias
                m = jnp.max(s, axis=-1, keepdims=True)
                p = jnp.exp2(s - m).astype(BF16)
                v1 = jnp.concatenate([vbuf[rr, keys, sl], ones], axis=1)
                ol = jnp.dot(p, v1, preferred_element_type=F32)
                o_ref[rr, rows, sl] = ol[:, :HEAD_DIM].astype(BF16)
                stats = jnp.where(lane == h, m, stats)
                stats = jnp.where(lane == N_HEADS + h, ol[:, HEAD_DIM:], stats)
            st_ref[rr, rows, :] = stats


def _attn_group(qkv):
    B, dil, sub_len, _ = qkv.shape
    width = N_HEADS * HEAD_DIM
    tl = min(ATTN_TL, sub_len)
    n_res = min(dil, ATTN_TL // tl)
    R = ATTN_REACH
    per = tl // R
    n_rblocks = sub_len // R

    def cur(c, w=width):
        return pl.BlockSpec((None, n_res, tl, w), lambda b, r, j: (b, r, j, c))

    def prev(c):
        return pl.BlockSpec((None, n_res, R, width),
                            lambda b, r, j: (b, r, jnp.maximum(j * per - 1, 0), c))

    def nxt(c):
        return pl.BlockSpec((None, n_res, R, width),
                            lambda b, r, j: (b, r, jnp.minimum((j + 1) * per, n_rblocks - 1), c))

    return pl.pallas_call(
        functools.partial(_attn_kernel, n_res, tl, sub_len),
        grid=(B, dil // n_res, sub_len // tl),
        in_specs=[cur(0), prev(1), cur(1), nxt(1), prev(2), cur(2), nxt(2)],
        out_specs=[cur(0), cur(0, LANES)],
        out_shape=[jax.ShapeDtypeStruct((B, dil, sub_len, width), BF16),
                   jax.ShapeDtypeStruct((B, dil, sub_len, LANES), F32)],
        scratch_shapes=[pltpu.VMEM((n_res, tl + 2 * R, width), BF16),
                        pltpu.VMEM((n_res, tl + 2 * R, width), BF16)],
        compiler_params=pltpu.CompilerParams(
            dimension_semantics=("arbitrary", "arbitrary", "arbitrary"), vmem_limit_bytes=VMEM_LIMIT),
        name=f"attn_d{dil}",
    )(qkv, qkv, qkv, qkv, qkv, qkv, qkv)


def _merge_steps(tm, x_ref, o0_ref, o1_ref, o2_ref, s0_ref, s1_ref, s2_ref, wo_ref, dst_ref, mbuf, obuf, sbuf,
                 wbuf):
    def regroup(gi, o_ref, s_ref):
        dil = GROUPS[gi + 1][1]
        for r in range(dil):
            rows = pl.ds(r, tm // dil, stride=dil)
            sbuf[gi, rows, :] = s_ref[r]
            for h in range(N_HEADS):
                obuf[gi, h, rows, :] = o_ref[r, :, h * HEAD_DIM:(h + 1) * HEAD_DIM].astype(F32)

    def weights():
        st = (s0_ref[0], sbuf[0], sbuf[1])
        m = jnp.maximum(jnp.maximum(st[0], st[1]), st[2])
        e = [jnp.exp2(s - m) for s in st]
        den = sum(eg * pltpu.roll(s, LANES - N_HEADS, 1) for eg, s in zip(e, st))
        lane = lax.broadcasted_iota(jnp.int32, den.shape, 1)
        inv = 1.0 / jnp.where(lane < N_HEADS, den, 1.0)
        for gi in range(N_GROUPS):
            wbuf[gi] = e[gi] * inv

    def head(h):
        sl = slice(h * HEAD_DIM, (h + 1) * HEAD_DIM)
        oh = (wbuf[0, :, h:h + 1] * o0_ref[0, :, sl].astype(F32) + wbuf[1, :, h:h + 1] * obuf[0, h]
              + wbuf[2, :, h:h + 1] * obuf[1, h])
        mbuf[:, sl] = oh.astype(BF16)

    def project():
        dst_ref[...] = x_ref[0] + jnp.dot(mbuf[...], wo_ref[...], preferred_element_type=F32)

    return ([functools.partial(regroup, 0, o1_ref, s1_ref), functools.partial(regroup, 1, o2_ref, s2_ref), weights]
            + [functools.partial(head, h) for h in range(N_HEADS)] + [project])


def _merge_ffn_kernel(tm, x_ref, o0_ref, o1_ref, o2_ref, s0_ref, s1_ref, s2_ref, wo_attn_ref,
                      g_ref, win_ref, wo_ref, out_ref, mbuf, obuf, sbuf, wbuf, xa_ref, xb_ref, xn_ref, acc_ref):
    i = pl.program_id(0)

    @pl.when(i == 0)
    def _():
        xb_ref[...] = jnp.zeros_like(xb_ref)

    def step(merged_ref, ready_ref):
        steps = _merge_steps(tm, x_ref, o0_ref, o1_ref, o2_ref, s0_ref, s1_ref, s2_ref, wo_attn_ref, merged_ref,
                             mbuf, obuf, sbuf, wbuf)
        first = lambda: (steps[0](), steps[1]())
        _ffn_layer(ready_ref, g_ref, win_ref, wo_ref, xn_ref, acc_ref, out_ref, side_work=[first] + steps[2:])

    @pl.when(i % 2 == 0)
    def _():
        step(xa_ref, xb_ref)

    @pl.when(i % 2 == 1)
    def _():
        step(xb_ref, xa_ref)


def _merge_ffn(x, outs, stats, w_o, ffn_layer):
    B, S, D = x.shape
    tm = min(MERGE_TM, S)
    nts = S // tm
    n_tiles = B * nts
    g, w_in, wo = ffn_layer

    def tile(i):
        s = jnp.minimum(i, n_tiles - 1)
        return s // nts, s % nts

    tok = lambda w: pl.BlockSpec((1, tm, w), lambda i: (*tile(i), 0))
    grp = lambda dil, w: pl.BlockSpec((None, dil, tm // dil, w), lambda i: (tile(i)[0], 0, tile(i)[1], 0))
    dils = [dil for _, dil in GROUPS]
    out = pl.pallas_call(
        functools.partial(_merge_ffn_kernel, tm),
        grid=(n_tiles + 1,),
        in_specs=([tok(D)] + [grp(dil, D) for dil in dils] + [grp(dil, LANES) for dil in dils]
                  + [_resident(a.shape) for a in (w_o, g, w_in, wo)]),
        out_specs=pl.BlockSpec((tm, D), lambda i: (jnp.maximum(i - 1, 0), 0)),
        out_shape=jax.ShapeDtypeStruct((B * S, D), F32),
        scratch_shapes=[pltpu.VMEM((tm, D), BF16),
                        pltpu.VMEM((N_GROUPS - 1, N_HEADS, tm, HEAD_DIM), F32),
                        pltpu.VMEM((N_GROUPS - 1, tm, LANES), F32),
                        pltpu.VMEM((N_GROUPS, tm, LANES), F32),
                        pltpu.VMEM((tm, D), F32), pltpu.VMEM((tm, D), F32),
                        pltpu.VMEM((tm, D), BF16), pltpu.VMEM((tm, D), F32)],
        compiler_params=pltpu.CompilerParams(
            dimension_semantics=("arbitrary",), vmem_limit_bytes=VMEM_LIMIT),
        name="attn_merge_ffn",
    )(x, *outs, *stats, w_o, g, w_in, wo)
    return out.reshape(B, S, D)


def _rope_tables(S):
    half = HEAD_DIM // 2
    lo_n = LANES
    inv = ROPE_THETA ** (-2.0 * jnp.arange(half, dtype=F32) / HEAD_DIM)
    ang_hi = (jnp.arange(S // lo_n, dtype=F32) * lo_n)[:, None] * inv[None, :]
    ang_lo = jnp.arange(lo_n, dtype=F32)[:, None] * inv[None, :]
    ch, sh = jnp.cos(ang_hi)[:, None, :], jnp.sin(ang_hi)[:, None, :]
    cl, sl = jnp.cos(ang_lo)[None, :, :], jnp.sin(ang_lo)[None, :, :]
    cos = (ch * cl - sh * sl).reshape(S, half)
    sin = (sh * cl + ch * sl).reshape(S, half)
    return jnp.concatenate([cos, cos], axis=-1), jnp.concatenate([-sin, sin], axis=-1)


def _prep_gates(gate_w, gate_b, d):
    gw = (0.5 * jnp.concatenate([gate_w[d, 0], gate_w[d, 1]], axis=-1)).astype(BF16)
    return gw, 0.5 * gate_b[d].reshape(2, LRU_WIDTH)


def _trunk(x, p):
    row = lambda v: v.reshape(1, -1)

    def ffn_layer(i, k):
        return (row(p["norm_g"][i, k * 2]), *p["ffn"][i][k])

    xs = _ffn(x, [ffn_layer(0, 0)], interleave=True)
    lru_common = (row(p["norm_g"][0, 1]), p["lru_w_in"], row(p["lru_b_in"]))
    hb, u = _lru_pass(xs, True, *lru_common, *p["lru_gates"][1], row(p["lru_lambda"][1]),
                      conv_w=p["lru_conv_w"], conv_b=row(p["lru_conv_b"]))
    x = _lru_pass(xs, False, *lru_common, *p["lru_gates"][0], row(p["lru_lambda"][0]),
                  u=u, hb=hb, w_out=p["lru_w_out"], b_out=row(p["lru_b_out"]))
    x = _ffn(x, [ffn_layer(0, 1), ffn_layer(1, 0)])
    qkv = _qkv(x, row(p["norm_g"][1, 1]), p["attn_w_qkv"], p["attn_q_gain"], p["attn_k_gain"], *p["rope"])
    outs, stats = zip(*[_attn_group(t) for t in qkv])
    return _merge_ffn(x, outs, stats, p["attn_w_o"], ffn_layer(1, 1))


def kernel(x_prompt, x_sample, norm_g, ffn_w_in, ffn_w_out, lru_w_in, lru_b_in, lru_conv_w, lru_conv_b, lru_gate_w, lru_gate_b, lru_lambda, lru_w_out, lru_b_out, attn_w_qkv, attn_q_gain, attn_k_gain, attn_w_o):
    p = {
        "norm_g": norm_g,
        "ffn": [[(ffn_w_in[i, k].astype(BF16), ffn_w_out[i, k].astype(BF16)) for k in range(2)]
                for i in range(2)],
        "lru_w_in": lru_w_in[0].astype(BF16),
        "lru_b_in": lru_b_in[0],
        "lru_conv_w": lru_conv_w[0],
        "lru_conv_b": lru_conv_b[0],
        "lru_gates": [_prep_gates(lru_gate_w[0], lru_gate_b[0], d) for d in range(2)],
        "lru_lambda": lru_lambda[0],
        "lru_w_out": lru_w_out[0].astype(BF16),
        "lru_b_out": lru_b_out[0],
        "attn_w_qkv": attn_w_qkv[0].astype(BF16),
        "attn_q_gain": attn_q_gain[0],
        "attn_k_gain": attn_k_gain[0],
        "attn_w_o": attn_w_o[0].astype(BF16),
        "rope": _rope_tables(max(x_prompt.shape[1], x_sample.shape[1])),
    }
    return _trunk(x_prompt, p), _trunk(x_sample, p)
```
